```python
import math
import jax, jax.numpy as jnp
from jax import lax
import numpy as np

D_MODEL = 1024
BATCH = 2
SEQ = 8192
DEPTH = 2

N_HEADS = 8
QK_NOPE_DIM = 64
QK_ROPE_DIM = 32
QK_HEAD_DIM = QK_NOPE_DIM + QK_ROPE_DIM
V_HEAD_DIM = 64
Q_LORA_RANK = 256
KV_LORA_RANK = 128
ROPE_THETA = 10000.0
Q_BLOCK = 128
ATTN_WIDTH = N_HEADS * V_HEAD_DIM

S5_WIDTH = 512
S5_GROUP = 16
S5_GROUPS = S5_WIDTH // S5_GROUP
S5_STATE = 64
DT_MIN = 1e-3
DT_MAX = 1e-1

D_FF_DENSE = 2816
N_EXPERTS = 8
TOP_K = 2
D_FF_EXPERT = 3584
N_DENSE = (DEPTH + 1) // 2
N_MOE = DEPTH // 2

EPS = 1e-6

IN_PARTS = [Q_LORA_RANK, KV_LORA_RANK, QK_ROPE_DIM, S5_WIDTH, D_MODEL, D_MODEL]
IN_COLS = sum(IN_PARTS)
IN_SPLITS = [int(v) for v in np.cumsum(IN_PARTS)[:-1]]

kernel_name = "hybrid_mla_s5_gated_moe_block"


def rmsnorm(x, g):
    xf = x.astype(jnp.float32)
    y = xf * lax.rsqrt(jnp.mean(xf * xf, axis=-1, keepdims=True) + EPS) * g.astype(jnp.float32)
    return y.astype(x.dtype)


def rope_tables(s, dtype):
    pos = jnp.arange(s, dtype=jnp.float32)
    inv = jnp.power(ROPE_THETA, -jnp.arange(0, QK_ROPE_DIM, 2, dtype=jnp.float32) / QK_ROPE_DIM)
    ang = pos[:, None] * inv[None, :]
    return jnp.cos(ang)[:, None, :].astype(dtype), jnp.sin(ang)[:, None, :].astype(dtype)


def rope(x, cos, sin):
    half = QK_ROPE_DIM // 2
    x1, x2 = x[..., :half], x[..., half:]
    return jnp.concatenate([x1 * cos - x2 * sin, x2 * cos + x1 * sin], axis=-1)


def causal_block_attention(q, k, v):
    b, s, h, dq = q.shape
    nb = s // Q_BLOCK
    qb = q.reshape(b, nb, Q_BLOCK, h, dq).transpose(1, 0, 2, 3, 4)
    kpos = jnp.arange(s)
    scale = QK_HEAD_DIM ** -0.5

    def one_block(args):
        qi, i = args
        qpos = i * Q_BLOCK + jnp.arange(Q_BLOCK)
        sc = jnp.einsum('bqhd,bkhd->bhqk', qi, k, preferred_element_type=jnp.float32) * scale
        sc = jnp.where(kpos[None, :] <= qpos[:, None], sc, -jnp.inf)
        p = jax.nn.softmax(sc, axis=-1).astype(v.dtype)
        return jnp.einsum('bhqk,bkhd->bqhd', p, v)

    out = lax.map(one_block, (qb, jnp.arange(nb)))
    return out.transpose(1, 0, 2, 3, 4).reshape(b, s, h, v.shape[-1])


def mla_branch(q_lat, kv_lat, k_rope, q_norm_g, w_uq, kv_norm_g, w_ukv, q_head_g, k_head_g):
    b, s, _ = q_lat.shape
    q = (rmsnorm(q_lat, q_norm_g) @ w_uq).reshape(b, s, N_HEADS, QK_HEAD_DIM)
    kv = (rmsnorm(kv_lat, kv_norm_g) @ w_ukv).reshape(b, s, N_HEADS, QK_NOPE_DIM + V_HEAD_DIM)
    k_nope, v = kv[..., :QK_NOPE_DIM], kv[..., QK_NOPE_DIM:]
    k_rope_h = jnp.broadcast_to(k_rope[:, :, None, :], (b, s, N_HEADS, QK_ROPE_DIM))
    k = jnp.concatenate([k_nope, k_rope_h], axis=-1)
    q = rmsnorm(q, q_head_g)
    k = rmsnorm(k, k_head_g)
    cos, sin = rope_tables(s, q.dtype)
    q = jnp.concatenate([q[..., :QK_NOPE_DIM], rope(q[..., QK_NOPE_DIM:], cos, sin)], axis=-1)
    k = jnp.concatenate([k[..., :QK_NOPE_DIM], rope(k[..., QK_NOPE_DIM:], cos, sin)], axis=-1)
    out = causal_block_attention(q, k, v)
    return out.reshape(b, s, ATTN_WIDTH)


def s5_branch(u, lam_re, lam_im, log_step, b_re, b_im, c_re, c_im, d_skip, w_glu, b_glu):
    bsz, s, _ = u.shape
    uf = u.astype(jnp.float32)
    ug = uf.reshape(bsz, s, S5_GROUPS, S5_GROUP).astype(jnp.complex64)
    lam = lax.complex(lam_re.astype(jnp.float32), lam_im.astype(jnp.float32))
    step = jnp.exp(log_step.astype(jnp.float32))[:, None]
    lam_bar = jnp.exp(lam * step)
    b_mat = lax.complex(b_re.astype(jnp.float32), b_im.astype(jnp.float32))
    b_bar = ((lam_bar - 1.0) / lam)[..., None] * b_mat
    bu = jnp.einsum('gpc,bsgc->bsgp', b_bar, ug)
    a = jnp.broadcast_to(lam_bar, bu.shape)

    def combine(left, right):
        a1, b1 = left
        a2, b2 = right
        return a2 * a1, a2 * b1 + b2

    _, states = lax.associative_scan(combine, (a, bu), axis=1)
    c_mat = lax.complex(c_re.astype(jnp.float32), c_im.astype(jnp.float32))
    y = jnp.real(jnp.einsum('gcp,bsgp->bsgc', c_mat, states)).reshape(bsz, s, S5_WIDTH)
    y = (y + d_skip.astype(jnp.float32) * uf).astype(u.dtype)
    y = jax.nn.gelu(y)
    return y * jax.nn.sigmoid(y @ w_glu + b_glu)


def swiglu(h, w_gate, w_up, w_down):
    return (jax.nn.silu(h @ w_gate) * (h @ w_up)) @ w_down


def moe_swiglu(h, router_w, router_b, w_gate, w_up, w_down):
    b, s, d = h.shape
    t = h.reshape(b * s, d)
    logits = (t @ router_w).astype(jnp.float32) + router_b.astype(jnp.float32)
    top_val, top_idx = lax.top_k(logits, TOP_K)
    top_w = jax.nn.softmax(top_val, axis=-1)
    comb = jnp.sum(jax.nn.one_hot(top_idx, N_EXPERTS, dtype=jnp.float32) * top_w[..., None], axis=1)
    comb = comb.astype(t.dtype)
    out = jnp.zeros_like(t)
    for e in range(N_EXPERTS):
        out = out + comb[:, e:e + 1] * swiglu(t, w_gate[e], w_up[e], w_down[e])
    return out.reshape(b, s, d)


def setup_inputs(seed: int = 0) -> dict:
    key = jax.random.key(seed)
    ks = iter(jax.random.split(key, 48))
    f32 = jnp.float32

    def nrm(shape, fan_in):
        return jax.random.normal(next(ks), shape, f32) * (fan_in ** -0.5)

    def gain(shape):
        return 1.0 + 0.02 * jax.random.normal(next(ks), shape, f32)

    def small(shape, scale=0.01):
        return scale * jax.random.normal(next(ks), shape, f32)

    L, G, P, C = DEPTH, S5_GROUPS, S5_STATE, S5_GROUP
    lam_re = -0.5 + small((L, G, P))
    lam_im = math.pi * jnp.broadcast_to(jnp.arange(P, dtype=f32), (L, G, P)) + small((L, G, P))
    log_step = jax.random.uniform(next(ks), (L, G), f32, math.log(DT_MIN), math.log(DT_MAX))
    return {
        "x": jax.random.normal(next(ks), (BATCH, SEQ, D_MODEL), f32),
        "ln_mix_g": gain((L, D_MODEL)),
        "w_in": nrm((L, D_MODEL, IN_COLS), D_MODEL),
        "b_in": small((L, IN_COLS)),
        "q_norm_g": gain((L, Q_LORA_RANK)),
        "w_uq": nrm((L, Q_LORA_RANK, N_HEADS * QK_HEAD_DIM), Q_LORA_RANK),
        "kv_norm_g": gain((L, KV_LORA_RANK)),
        "w_ukv": nrm((L, KV_LORA_RANK, N_HEADS * (QK_NOPE_DIM + V_HEAD_DIM)), KV_LORA_RANK),
        "q_head_g": gain((L, QK_HEAD_DIM)),
        "k_head_g": gain((L, QK_HEAD_DIM)),
        "s5_lam_re": lam_re,
        "s5_lam_im": lam_im,
        "s5_log_step": log_step,
        "s5_b_re": nrm((L, G, P, C), 2 * C),
        "s5_b_im": nrm((L, G, P, C), 2 * C),
        "s5_c_re": nrm((L, G, C, P), P),
        "s5_c_im": nrm((L, G, C, P), P),
        "s5_d": jax.random.normal(next(ks), (L, S5_WIDTH), f32),
        "w_glu": nrm((L, S5_WIDTH, S5_WIDTH), S5_WIDTH),
        "b_glu": small((L, S5_WIDTH)),
        "w_proj_attn": nrm((L, ATTN_WIDTH, D_MODEL), ATTN_WIDTH),
        "w_proj_ssm": nrm((L, S5_WIDTH, D_MODEL), S5_WIDTH),
        "w_out": nrm((L, D_MODEL, D_MODEL), D_MODEL),
        "ln_ffn_g": gain((L, D_MODEL)),
        "ffn_w_gate": nrm((N_DENSE, D_MODEL, D_FF_DENSE), D_MODEL),
        "ffn_w_up": nrm((N_DENSE, D_MODEL, D_FF_DENSE), D_MODEL),
        "ffn_w_down": nrm((N_DENSE, D_FF_DENSE, D_MODEL), D_FF_DENSE),
        "router_w": nrm((N_MOE, D_MODEL, N_EXPERTS), D_MODEL),
        "router_b": small((N_MOE, N_EXPERTS)),
        "moe_w_gate": nrm((N_MOE, N_EXPERTS, D_MODEL, D_FF_EXPERT), D_MODEL),
        "moe_w_up": nrm((N_MOE, N_EXPERTS, D_MODEL, D_FF_EXPERT), D_MODEL),
        "moe_w_down": nrm((N_MOE, N_EXPERTS, D_FF_EXPERT, D_MODEL), D_FF_EXPERT),
    }


def reference(x, ln_mix_g, w_in, b_in, q_norm_g, w_uq, kv_norm_g, w_ukv, q_head_g, k_head_g,
              s5_lam_re, s5_lam_im, s5_log_step, s5_b_re, s5_b_im, s5_c_re, s5_c_im, s5_d,
              w_glu, b_glu, w_proj_attn, w_proj_ssm, w_out, ln_ffn_g,
              ffn_w_gate, ffn_w_up, ffn_w_down, router_w, router_b,
              moe_w_gate, moe_w_up, moe_w_down):
    for l in range(DEPTH):
        h = rmsnorm(x, ln_mix_g[l])
        proj = h @ w_in[l] + b_in[l]
        q_lat, kv_lat, k_rope, u, gate_a, gate_b = jnp.split(proj, IN_SPLITS, axis=-1)
        attn = mla_branch(q_lat, kv_lat, k_rope, q_norm_g[l], w_uq[l], kv_norm_g[l], w_ukv[l],
                          q_head_g[l], k_head_g[l])
        ssm = s5_branch(u, s5_lam_re[l], s5_lam_im[l], s5_log_step[l], s5_b_re[l], s5_b_im[l],
                        s5_c_re[l], s5_c_im[l], s5_d[l], w_glu[l], b_glu[l])
        mix = (jax.nn.sigmoid(gate_a) * (attn @ w_proj_attn[l])
               + jax.nn.sigmoid(gate_b) * (ssm @ w_proj_ssm[l]))
        x = x + mix @ w_out[l]
        h = rmsnorm(x, ln_ffn_g[l])
        if l % 2 == 0:
            j = l // 2
            x = x + swiglu(h, ffn_w_gate[j], ffn_w_up[j], ffn_w_down[j])
        else:
            j = l // 2
            x = x + moe_swiglu(h, router_w[j], router_b[j], moe_w_gate[j], moe_w_up[j], moe_w_down[j])
    return x
```

```python
import functools
import math

import jax
import jax.numpy as jnp
from jax import lax
from jax.experimental import pallas as pl
from jax.experimental.pallas import tpu as pltpu

EPS = 1e-6
N_HEADS = 8
QK_NOPE_DIM = 64
QK_ROPE_DIM = 32
QK_HEAD_DIM = QK_NOPE_DIM + QK_ROPE_DIM
V_HEAD_DIM = 64
HEAD_PAD = 128
Q_LORA_RANK = 256
KV_LORA_RANK = 128
ROPE_THETA = 10000.0
S5_WIDTH = 512
S5_GROUP = 16
S5_GROUPS = S5_WIDTH // S5_GROUP
S5_STATE = 64
S5_CHUNK = 16
N_EXPERTS = 8
ROUTER_PAD = 128
NEG_BIG = -1e30

VMEM_LIMIT_BYTES = 56 * 1024 * 1024

F32 = jnp.float32
BF16 = jnp.bfloat16


def _dot(a, b):
    return jnp.dot(a, b, preferred_element_type=F32)


def _rms(x, g):
    return x * lax.rsqrt(jnp.mean(x * x, axis=-1, keepdims=True) + EPS) * g


def _params(*semantics):
    return pltpu.CompilerParams(dimension_semantics=semantics, vmem_limit_bytes=VMEM_LIMIT_BYTES)


def _full(shape):
    return pl.BlockSpec(shape, lambda *_: (0,) * len(shape))


def _inproj_kernel(x_ref, g_ref, w1_ref, b1_ref, w2_ref, b2_ref, qng_ref, wuq_ref, kvng_ref,
                   wk_ref, wv_ref, qhg_ref, khg_ref, cos_ref, sa_ref, sb_ref,
                   q_ref, k_ref, v_ref, u_ref, ga_ref, gb_ref):
    d = x_ref.shape[1]
    h = _rms(x_ref[...], g_ref[...]).astype(BF16)
    p1 = _dot(h, w1_ref[...]) + b1_ref[...]
    p2 = _dot(h, w2_ref[...]) + b2_ref[...]
    ga_ref[...] = jax.nn.sigmoid(p2[:, :d]).astype(BF16)
    gb_ref[...] = jax.nn.sigmoid(p2[:, d:]).astype(BF16)
    c0 = Q_LORA_RANK
    c1 = c0 + KV_LORA_RANK
    c2 = c1 + HEAD_PAD
    u_ref[...] = p1[:, c2:]
    qn = _rms(p1[:, :c0], qng_ref[...]).astype(BF16)
    kvn = _rms(p1[:, c0:c1], kvng_ref[...]).astype(BF16)
    k_rope = p1[:, c1:c2]
    q_raw = _dot(qn, wuq_ref[...])
    k_raw = _dot(kvn, wk_ref[...])
    v_ref[...] = _dot(kvn, wv_ref[...]).astype(BF16)
    cos = cos_ref[...]
    sa = sa_ref[...]
    sb = sb_ref[...]
    qhg = qhg_ref[...]
    khg = khg_ref[...]
    scale = QK_HEAD_DIM ** -0.5

    def norm_rope(t, g):
        ms = jnp.sum(t * t, axis=-1, keepdims=True) * (1.0 / QK_HEAD_DIM)
        t = t * lax.rsqrt(ms + EPS) * g
        return t * cos + pltpu.roll(t, HEAD_PAD - 16, 1) * sa + pltpu.roll(t, 16, 1) * sb

    for hd in range(N_HEADS):
        sl = slice(hd * HEAD_PAD, (hd + 1) * HEAD_PAD)
        q_ref[:, sl] = (norm_rope(q_raw[:, sl], qhg) * scale).astype(BF16)
        k_ref[:, sl] = norm_rope(k_raw[:, sl] + k_rope, khg).astype(BF16)


def _inproj(x2d, lp, rope, seq, tm):
    t, d = x2d.shape
    nt = t // tm
    nseq = seq // tm
    hp = N_HEADS * HEAD_PAD
    row = lambda i: (i, 0)
    pos = lambda i: (i % nseq, 0)
    in_specs = [
        pl.BlockSpec((tm, d), row), _full((1, d)),
        _full(lp["w1"].shape), _full(lp["b1"].shape), _full(lp["w2"].shape), _full(lp["b2"].shape),
        _full((1, Q_LORA_RANK)), _full(lp["wuq"].shape), _full((1, KV_LORA_RANK)),
        _full(lp["wk"].shape), _full(lp["wv"].shape), _full((1, HEAD_PAD)), _full((1, HEAD_PAD)),
        pl.BlockSpec((tm, HEAD_PAD), pos), pl.BlockSpec((tm, HEAD_PAD), pos),
        pl.BlockSpec((tm, HEAD_PAD), pos),
    ]
    out_shape = (
        jax.ShapeDtypeStruct((t, hp), BF16), jax.ShapeDtypeStruct((t, hp), BF16),
        jax.ShapeDtypeStruct((t, N_HEADS * V_HEAD_DIM), BF16),
        jax.ShapeDtypeStruct((t, S5_WIDTH), F32),
        jax.ShapeDtypeStruct((t, d), BF16), jax.ShapeDtypeStruct((t, d), BF16),
    )
    out_specs = (
        pl.BlockSpec((tm, hp), row), pl.BlockSpec((tm, hp), row),
        pl.BlockSpec((tm, N_HEADS * V_HEAD_DIM), row), pl.BlockSpec((tm, S5_WIDTH), row),
        pl.BlockSpec((tm, d), row), pl.BlockSpec((tm, d), row),
    )
    return pl.pallas_call(
        _inproj_kernel, grid=(nt,), in_specs=in_specs, out_specs=out_specs, out_shape=out_shape,
        compiler_params=_params("parallel"), name="inproj",
    )(x2d, lp["ln_mix_g"], lp["w1"], lp["b1"], lp["w2"], lp["b2"], lp["q_norm_g"], lp["wuq"],
      lp["kv_norm_g"], lp["wk"], lp["wv"], lp["q_head_g"], lp["k_head_g"], *rope)


def _attn_kernel(q_ref, k_ref, v_ref, o_ref, m_scr, l_scr, acc_scr):
    i = pl.program_id(1)
    j = pl.program_id(2)
    tq = q_ref.shape[0]
    tk = k_ref.shape[0]

    @pl.when(j == 0)
    def _():
        m_scr[...] = jnp.full(m_scr.shape, -jnp.inf, F32)
        l_scr[...] = jnp.zeros(l_scr.shape, F32)
        acc_scr[...] = jnp.zeros(acc_scr.shape, F32)

    def step(masked):
        if masked:
            rows = lax.broadcasted_iota(jnp.int32, (tq, tk), 0)
            cols = lax.broadcasted_iota(jnp.int32, (tq, tk), 1)
            keep = cols <= rows
        for hd in range(N_HEADS):
            sl = slice(hd * HEAD_PAD, (hd + 1) * HEAD_PAD)
            s = lax.dot_general(q_ref[:, sl], k_ref[:, sl], (((1,), (1,)), ((), ())),
                                preferred_element_type=F32)
            if masked:
                s = jnp.where(keep, s, -jnp.inf)
            m_prev = m_scr[hd]
            m_new = jnp.maximum(m_prev, jnp.max(s, axis=-1, keepdims=True))
            alpha = jnp.exp(m_prev - m_new)
            p = jnp.exp(s - m_new)
            l_scr[hd] = alpha * l_scr[hd] + jnp.sum(p, axis=-1, keepdims=True)
            pv = _dot(p.astype(BF16), v_ref[:, (hd // 2) * HEAD_PAD:(hd // 2 + 1) * HEAD_PAD])
            acc_scr[hd] = alpha * acc_scr[hd] + pv
            m_scr[hd] = m_new

    @pl.when(j < i)
    def _():
        step(False)

    @pl.when(j == i)
    def _():
        step(True)
        lane = lax.broadcasted_iota(jnp.int32, (tq, HEAD_PAD), 1)
        for pr in range(N_HEADS // 2):
            lo = acc_scr[2 * pr] / l_scr[2 * pr]
            hi = acc_scr[2 * pr + 1] / l_scr[2 * pr + 1]
            o_ref[:, pr * HEAD_PAD:(pr + 1) * HEAD_PAD] = jnp.where(lane < V_HEAD_DIM, lo, hi).astype(BF16)


def _attention(q, k, v, batch, seq, tq):
    nq = seq // tq
    hp = N_HEADS * HEAD_PAD
    vw = N_HEADS * V_HEAD_DIM
    qmap = lambda b, i, j: (b * nq + i, 0)
    kmap = lambda b, i, j: (b * nq + jnp.minimum(j, i), 0)
    return pl.pallas_call(
        _attn_kernel, grid=(batch, nq, nq),
        in_specs=[pl.BlockSpec((tq, hp), qmap), pl.BlockSpec((tq, hp), kmap),
                  pl.BlockSpec((tq, vw), kmap)],
        out_specs=pl.BlockSpec((tq, vw), qmap),
        out_shape=jax.ShapeDtypeStruct((batch * seq, vw), BF16),
        scratch_shapes=[pltpu.VMEM((N_HEADS, tq, 1), F32), pltpu.VMEM((N_HEADS, tq, 1), F32),
                        pltpu.VMEM((N_HEADS, tq, HEAD_PAD), F32)],
        compiler_params=_params("parallel", "parallel", "arbitrary"), name="attention",
    )(q, k, v)


def _s5_kernel(u_ref, m_ref, p_ref, q_ref, a_ref, y_ref, *, chunks_per_seq):
    ug = u_ref[0]
    rows = ug.shape[0]
    half = S5_STATE
    z = _dot(ug, p_ref[0])
    row_in_seq = lax.broadcasted_iota(jnp.int32, (rows, 2 * half), 0) & (chunks_per_seq - 1)
    a = a_ref[0]
    off = 1
    lvl = 0
    while off < chunks_per_seq:
        prev = pltpu.roll(z, off, 0)
        contrib = prev * a[2 * lvl:2 * lvl + 1] + pltpu.roll(prev, half, 1) * a[2 * lvl + 1:2 * lvl + 2]
        z = z + jnp.where(row_in_seq >= off, contrib, 0.0)
        off *= 2
        lvl += 1
    x_start = jnp.where(row_in_seq >= 1, pltpu.roll(z, 1, 0), 0.0)
    y_ref[0] = _dot(ug, m_ref[0]) + _dot(x_start.astype(BF16), q_ref[0])


def _s5_scan(ug, mats, chunks_per_seq):
    g, rows, w = ug.shape
    m, p, q, a = mats
    blk = lambda shape: pl.BlockSpec((1,) + shape, lambda i: (i, 0, 0))
    return pl.pallas_call(
        functools.partial(_s5_kernel, chunks_per_seq=chunks_per_seq), grid=(g,),
        in_specs=[blk((rows, w)), blk(m.shape[1:]), blk(p.shape[1:]), blk(q.shape[1:]), blk(a.shape[1:])],
        out_specs=blk((rows, w)), out_shape=jax.ShapeDtypeStruct((g, rows, w), F32),
        compiler_params=_params("parallel"), name="s5_scan",
    )(ug, m, p, q, a)


def _s5_matrices(lam_re, lam_im, log_step, b_re, b_im, c_re, c_im, chunks_per_seq):
    hi = lax.Precision.HIGHEST
    ln = S5_CHUNK
    step = jnp.exp(log_step)[:, None]
    ar = lam_re * step
    ai = lam_im * step
    lbr = jnp.exp(ar) * jnp.cos(ai)
    lbi = jnp.exp(ar) * jnp.sin(ai)
    den = lam_re * lam_re + lam_im * lam_im
    cr = ((lbr - 1.0) * lam_re + lbi * lam_im) / den
    ci = (lbi * lam_re - (lbr - 1.0) * lam_im) / den
    bbr = cr[..., None] * b_re - ci[..., None] * b_im
    bbi = cr[..., None] * b_im + ci[..., None] * b_re
    dd = jnp.arange(ln + 1, dtype=F32)[:, None, None]
    pr = jnp.exp(dd * ar) * jnp.cos(dd * ai)
    pi = jnp.exp(dd * ar) * jnp.sin(dd * ai)
    tr = pr[:ln, :, :, None] * bbr - pi[:ln, :, :, None] * bbi
    ti = pr[:ln, :, :, None] * bbi + pi[:ln, :, :, None] * bbr
    kd = (jnp.einsum("gcp,dgpe->dgce", c_re, tr, precision=hi)
          - jnp.einsum("gcp,dgpe->dgce", c_im, ti, precision=hi))
    lag = jnp.arange(ln)[None, :] - jnp.arange(ln)[:, None]
    toe = jnp.where((lag >= 0)[:, :, None, None, None], kd[jnp.clip(lag, 0, ln - 1)], 0.0)
    m = toe.transpose(2, 0, 4, 1, 3).reshape(S5_GROUPS, ln * S5_GROUP, ln * S5_GROUP)
    p_re = tr[::-1].transpose(1, 0, 3, 2).reshape(S5_GROUPS, ln * S5_GROUP, S5_STATE)
    p_im = ti[::-1].transpose(1, 0, 3, 2).reshape(S5_GROUPS, ln * S5_GROUP, S5_STATE)
    p = jnp.concatenate([p_re, p_im], axis=-1)
    q_re = c_re[None] * pr[1:, :, None, :] - c_im[None] * pi[1:, :, None, :]
    q_im = -(c_re[None] * pi[1:, :, None, :] + c_im[None] * pr[1:, :, None, :])
    q = jnp.concatenate([q_re.transpose(1, 3, 0, 2).reshape(S5_GROUPS, S5_STATE, ln * S5_GROUP),
                         q_im.transpose(1, 3, 0, 2).reshape(S5_GROUPS, S5_STATE, ln * S5_GROUP)], axis=1)
    levels = int(math.log2(chunks_per_seq))
    e = (ln * (2.0 ** jnp.arange(levels, dtype=F32)))[:, None, None]
    er = jnp.exp(e * ar) * jnp.cos(e * ai)
    ei = jnp.exp(e * ar) * jnp.sin(e * ai)
    a = jnp.stack([jnp.concatenate([er, er], -1), jnp.concatenate([-ei, ei], -1)], axis=1)
    a = a.transpose(2, 0, 1, 3).reshape(S5_GROUPS, 2 * levels, 2 * S5_STATE)
    return m.astype(BF16), p.astype(BF16), q.astype(BF16), a


def _mix_kernel(*refs, routed):
    (x_ref, attn_ref, ys_ref, u_ref, ga_ref, gb_ref, d_ref, wglu_ref, bglu_ref, wpa_ref, wps_ref,
     wout_ref, lng_ref) = refs[:13]
    if routed:
        rw_ref, rb_ref, xo_ref, h_ref, comb_ref, sel_ref = refs[13:]
    else:
        xo_ref, h_ref = refs[13:]
    y = ys_ref[...] + d_ref[...] * u_ref[...]
    y = jax.nn.gelu(y, approximate=True)
    ssm = y * jax.nn.sigmoid(_dot(y.astype(BF16), wglu_ref[...]) + bglu_ref[...])
    pa = _dot(attn_ref[...], wpa_ref[...])
    ps = _dot(ssm.astype(BF16), wps_ref[...])
    mix = ga_ref[...].astype(F32) * pa + gb_ref[...].astype(F32) * ps
    xn = x_ref[...] + _dot(mix.astype(BF16), wout_ref[...])
    xo_ref[...] = xn
    h2 = _rms(xn, lng_ref[...])
    h_ref[...] = h2.astype(BF16)
    if routed:
        logits = jnp.dot(h2, rw_ref[...], preferred_element_type=F32,
                         precision=lax.Precision.HIGHEST) + rb_ref[...]
        lane = lax.broadcasted_iota(jnp.int32, logits.shape, 1).astype(F32)
        m1 = jnp.max(logits, axis=-1, keepdims=True)
        i1 = jnp.min(jnp.where(logits == m1, lane, float(ROUTER_PAD)), axis=-1, keepdims=True)
        pick1 = lane == i1
        rest = jnp.where(pick1, NEG_BIG, logits)
        m2 = jnp.max(rest, axis=-1, keepdims=True)
        i2 = jnp.min(jnp.where(rest == m2, lane, float(ROUTER_PAD)), axis=-1, keepdims=True)
        pick2 = lane == i2
        e2 = jnp.exp(m2 - m1)
        w1 = 1.0 / (1.0 + e2)
        comb_ref[...] = jnp.where(pick1, w1, 0.0) + jnp.where(pick2, e2 * w1, 0.0)
        sel_ref[...] = jnp.where(pick1, 1.0, 0.0) + jnp.where(pick2, 1.0, 0.0)


def _mix(x2d, attn, ys, u, ga, gb, lp, tm, router=None):
    t, d = x2d.shape
    row = lambda i: (i, 0)
    routed = router is not None
    in_specs = [
        pl.BlockSpec((tm, d), row), pl.BlockSpec((tm, attn.shape[1]), row),
        pl.BlockSpec((tm, S5_WIDTH), row), pl.BlockSpec((tm, S5_WIDTH), row),
        pl.BlockSpec((tm, d), row), pl.BlockSpec((tm, d), row),
        _full((1, S5_WIDTH)), _full(lp["w_glu"].shape), _full((1, S5_WIDTH)),
        _full(lp["w_proj_attn"].shape), _full(lp["w_proj_ssm"].shape), _full(lp["w_out"].shape),
        _full((1, d)),
    ]
    args = [x2d, attn, ys, u, ga, gb, lp["s5_d"], lp["w_glu"], lp["b_glu"], lp["w_proj_attn"],
            lp["w_proj_ssm"], lp["w_out"], lp["ln_ffn_g"]]
    out_shape = [jax.ShapeDtypeStruct((t, d), F32), jax.ShapeDtypeStruct((t, d), BF16)]
    out_specs = [pl.BlockSpec((tm, d), row), pl.BlockSpec((tm, d), row)]
    if routed:
        in_specs += [_full(router[0].shape), _full(router[1].shape)]
        args += list(router)
        out_shape += [jax.ShapeDtypeStruct((t, ROUTER_PAD), F32)] * 2
        out_specs += [pl.BlockSpec((tm, ROUTER_PAD), row)] * 2
    return pl.pallas_call(
        functools.partial(_mix_kernel, routed=routed), grid=(t // tm,), in_specs=in_specs,
        out_specs=tuple(out_specs), out_shape=tuple(out_shape),
        compiler_params=_params("parallel"), name="mix_routed" if routed else "mix",
    )(*args)


def _ffn_kernel(te_ref, nu_ref, *refs, residual):
    if residual:
        x_ref, wg_ref, wu_ref, wd_ref, rw_ref, res_ref, o_ref, acc_scr = refs
    else:
        x_ref, wg_ref, wu_ref, wd_ref, rw_ref, o_ref, acc_scr = refs
    i = pl.program_id(0)
    j = pl.program_id(1)
    last = pl.num_programs(1) - 1
    used = i < nu_ref[0]

    @pl.when(used)
    def _():
        x = x_ref[...]
        g = _dot(x, wg_ref[0])
        a = (g * jax.nn.sigmoid(g) * _dot(x, wu_ref[0])).astype(BF16)
        part = _dot(a, wd_ref[0])

        @pl.when(j == 0)
        def _():
            acc_scr[...] = part

        @pl.when(j > 0)
        def _():
            acc_scr[...] += part

    @pl.when(jnp.logical_and(used, j == last))
    def _():
        out = acc_scr[...] * rw_ref[...]
        if residual:
            out = out + res_ref[...]
        o_ref[...] = out.astype(o_ref.dtype)

    @pl.when(jnp.logical_and(jnp.logical_not(used), j == last))
    def _():
        o_ref[...] = jnp.zeros(o_ref.shape, o_ref.dtype)


def _grouped_ffn(tile_expert, n_used, xs, wg, wu, wd, row_w, res, tm, tf, out_dtype):
    n, d = xs.shape
    dff = wg.shape[2]
    nff = dff // tf
    residual = res is not None

    def jeff(i, j, nu):
        return jnp.where(i < nu[0], j, nff - 1)

    row = lambda i, j, te, nu: (i, 0)
    in_specs = [
        pl.BlockSpec((tm, d), row),
        pl.BlockSpec((1, d, tf), lambda i, j, te, nu: (te[i], 0, jeff(i, j, nu))),
        pl.BlockSpec((1, d, tf), lambda i, j, te, nu: (te[i], 0, jeff(i, j, nu))),
        pl.BlockSpec((1, tf, d), lambda i, j, te, nu: (te[i], jeff(i, j, nu), 0)),
        pl.BlockSpec((tm, 1), row),
    ]
    args = [xs, wg, wu, wd, row_w]
    if residual:
        in_specs.append(pl.BlockSpec((tm, d), row))
        args.append(res)
    grid_spec = pltpu.PrefetchScalarGridSpec(
        num_scalar_prefetch=2, grid=(n // tm, nff), in_specs=in_specs,
        out_specs=pl.BlockSpec((tm, d), row), scratch_shapes=[pltpu.VMEM((tm, d), F32)])
    return pl.pallas_call(
        functools.partial(_ffn_kernel, residual=residual), grid_spec=grid_spec,
        out_shape=jax.ShapeDtypeStruct((n, d), out_dtype),
        compiler_params=_params("parallel", "arbitrary"),
        name="ffn_dense" if residual else "ffn_experts",
    )(tile_expert, n_used, *args)


def _rope_tables(seq):
    half = QK_ROPE_DIM // 2
    pos = jnp.arange(seq, dtype=F32)
    inv = jnp.power(ROPE_THETA, -jnp.arange(0, QK_ROPE_DIM, 2, dtype=F32) / QK_ROPE_DIM)
    ang = pos[:, None] * inv[None, :]
    cos, sin = jnp.cos(ang), jnp.sin(ang)
    ones = jnp.ones((seq, QK_NOPE_DIM), F32)
    zeros = jnp.zeros((seq, half), F32)
    tail = jnp.zeros((seq, HEAD_PAD - QK_HEAD_DIM), F32)
    znope = jnp.zeros((seq, QK_NOPE_DIM), F32)
    c = jnp.concatenate([ones, cos, cos, tail], axis=1)
    sa = jnp.concatenate([znope, -sin, zeros, tail], axis=1)
    sb = jnp.concatenate([znope, zeros, sin, tail], axis=1)
    return c, sa, sb


def _pad_heads(w, per_head, place_at=0):
    k = w.shape[0]
    w = w.reshape(k, N_HEADS, per_head)
    w = jnp.pad(w, ((0, 0), (0, 0), (place_at, HEAD_PAD - per_head - place_at)))
    return w.reshape(k, N_HEADS * HEAD_PAD)


def _layer_params(l, p):
    d = p["w_in"].shape[1]
    w_in, b_in = p["w_in"][l], p["b_in"][l]
    o = [0, Q_LORA_RANK, Q_LORA_RANK + KV_LORA_RANK, Q_LORA_RANK + KV_LORA_RANK + QK_ROPE_DIM]
    o.append(o[3] + S5_WIDTH)
    o.append(o[4] + d)
    pad_rope = lambda a: jnp.pad(a, ((0, 0), (QK_NOPE_DIM, HEAD_PAD - QK_HEAD_DIM)))
    w1 = jnp.concatenate([w_in[:, o[0]:o[2]], pad_rope(w_in[:, o[2]:o[3]]), w_in[:, o[3]:o[4]]], axis=1)
    b1 = jnp.concatenate([b_in[None, o[0]:o[2]], pad_rope(b_in[None, o[2]:o[3]]), b_in[None, o[3]:o[4]]], axis=1)
    w_ukv = p["w_ukv"][l].reshape(KV_LORA_RANK, N_HEADS, QK_NOPE_DIM + V_HEAD_DIM)
    pad_g = lambda g: jnp.pad(g[None, :], ((0, 0), (0, HEAD_PAD - QK_HEAD_DIM)))
    return {
        "ln_mix_g": p["ln_mix_g"][l][None], "w1": w1.astype(BF16), "b1": b1,
        "w2": w_in[:, o[4]:].astype(BF16), "b2": b_in[None, o[4]:],
        "q_norm_g": p["q_norm_g"][l][None], "wuq": _pad_heads(p["w_uq"][l], QK_HEAD_DIM).astype(BF16),
        "kv_norm_g": p["kv_norm_g"][l][None],
        "wk": _pad_heads(w_ukv[:, :, :QK_NOPE_DIM].reshape(KV_LORA_RANK, -1), QK_NOPE_DIM).astype(BF16),
        "wv": w_ukv[:, :, QK_NOPE_DIM:].reshape(KV_LORA_RANK, -1).astype(BF16),
        "q_head_g": pad_g(p["q_head_g"][l]), "k_head_g": pad_g(p["k_head_g"][l]),
        "s5_d": p["s5_d"][l][None], "w_glu": p["w_glu"][l].astype(BF16), "b_glu": p["b_glu"][l][None],
        "w_proj_attn": p["w_proj_attn"][l].astype(BF16), "w_proj_ssm": p["w_proj_ssm"][l].astype(BF16),
        "w_out": p["w_out"][l].astype(BF16), "ln_ffn_g": p["ln_ffn_g"][l][None],
    }


def _routing(sel, comb, tm):
    t = sel.shape[0]
    sel8 = sel[:, :N_EXPERTS] > 0.5
    seli = sel8.astype(jnp.int32)
    cnt = jnp.sum(seli, axis=0)
    padded = ((cnt + tm - 1) // tm) * tm
    ends = jnp.cumsum(padded)
    pos = (ends - padded)[None, :] + jnp.cumsum(seli, axis=0) - seli
    n_slots = 2 * t + N_EXPERTS * tm
    dest = jnp.where(sel8, pos, n_slots).reshape(-1)
    tok = jnp.broadcast_to(jnp.arange(t, dtype=jnp.int32)[:, None], (t, N_EXPERTS)).reshape(-1)
    tok_of_slot = jnp.zeros((n_slots,), jnp.int32).at[dest].set(tok, mode="drop")
    w_of_slot = jnp.zeros((n_slots,), F32).at[dest].set(comb[:, :N_EXPERTS].reshape(-1), mode="drop")
    tile_start = jnp.arange(n_slots // tm, dtype=jnp.int32) * tm
    tile_expert = jnp.sum((tile_start[:, None] >= ends[None, :]).astype(jnp.int32), axis=1)
    tile_expert = jnp.minimum(tile_expert, N_EXPERTS - 1)
    n_used = (ends[-1] // tm).astype(jnp.int32)[None]
    tile_expert = jnp.where(tile_start < ends[-1], tile_expert, tile_expert[jnp.maximum(n_used[0] - 1, 0)])
    slots = jnp.stack([jnp.min(jnp.where(sel8, pos, n_slots), axis=1),
                       jnp.max(jnp.where(sel8, pos, -1), axis=1)], axis=1)
    return tok_of_slot, w_of_slot[:, None], tile_expert.astype(jnp.int32), n_used, slots


def kernel(x, ln_mix_g, w_in, b_in, q_norm_g, w_uq, kv_norm_g, w_ukv, q_head_g, k_head_g,
           s5_lam_re, s5_lam_im, s5_log_step, s5_b_re, s5_b_im, s5_c_re, s5_c_im, s5_d,
           w_glu, b_glu, w_proj_attn, w_proj_ssm, w_out, ln_ffn_g,
           ffn_w_gate, ffn_w_up, ffn_w_down, router_w, router_b,
           moe_w_gate, moe_w_up, moe_w_down):
    batch, seq, d = x.shape
    depth = w_in.shape[0]
    t = batch * seq
    tm = min(512, seq)
    tq = min(512, seq)
    tm_ffn = min(1024, t)
    chunks_per_seq = seq // S5_CHUNK
    assert chunks_per_seq & (chunks_per_seq - 1) == 0 and seq % tm == 0 and t % tm_ffn == 0
    p = dict(ln_mix_g=ln_mix_g, w_in=w_in, b_in=b_in, q_norm_g=q_norm_g, w_uq=w_uq,
             kv_norm_g=kv_norm_g, w_ukv=w_ukv, q_head_g=q_head_g, k_head_g=k_head_g, s5_d=s5_d,
             w_glu=w_glu, b_glu=b_glu, w_proj_attn=w_proj_attn, w_proj_ssm=w_proj_ssm,
             w_out=w_out, ln_ffn_g=ln_ffn_g)
    rope = _rope_tables(seq)
    x2d = x.reshape(t, d)
    for l in range(depth):
        lp = _layer_params(l, p)
        q, k, v, u, ga, gb = _inproj(x2d, lp, rope, seq, tm)
        attn = _attention(q, k, v, batch, seq, tq)
        mats = _s5_matrices(s5_lam_re[l], s5_lam_im[l], s5_log_step[l], s5_b_re[l], s5_b_im[l],
                            s5_c_re[l], s5_c_im[l], chunks_per_seq)
        ug = (u.astype(BF16).reshape(t // S5_CHUNK, S5_CHUNK, S5_GROUPS, S5_GROUP)
              .transpose(2, 0, 1, 3).reshape(S5_GROUPS, t // S5_CHUNK, S5_CHUNK * S5_GROUP))
        yg = _s5_scan(ug, mats, chunks_per_seq)
        ys = (yg.reshape(S5_GROUPS, t // S5_CHUNK, S5_CHUNK, S5_GROUP)
              .transpose(1, 2, 0, 3).reshape(t, S5_WIDTH))
        j = l // 2
        if l % 2 == 0:
            x2d, h2 = _mix(x2d, attn, ys, u, ga, gb, lp, tm)
            n_tiles = t // tm_ffn
            x2d = _grouped_ffn(
                jnp.zeros((n_tiles,), jnp.int32), jnp.full((1,), n_tiles, jnp.int32), h2,
                ffn_w_gate[j][None].astype(BF16), ffn_w_up[j][None].astype(BF16),
                ffn_w_down[j][None].astype(BF16), jnp.ones((t, 1), F32), x2d, tm_ffn, 256, F32)
        else:
            rw = jnp.pad(router_w[j], ((0, 0), (0, ROUTER_PAD - N_EXPERTS)))
            rb = jnp.pad(router_b[j][None], ((0, 0), (0, ROUTER_PAD - N_EXPERTS)), constant_values=NEG_BIG)
            x2d, h2, comb, sel = _mix(x2d, attn, ys, u, ga, gb, lp, tm, router=(rw, rb))
            tok_of_slot, w_of_slot, tile_expert, n_used, slots = _routing(sel, comb, tm_ffn)
            xs = jnp.take(h2, tok_of_slot, axis=0)
            ysort = _grouped_ffn(tile_expert, n_used, xs, moe_w_gate[j].astype(BF16),
                                 moe_w_up[j].astype(BF16), moe_w_down[j].astype(BF16),
                                 w_of_slot, None, tm_ffn, 512, F32)
            x2d = x2d + jnp.take(ysort, slots[:, 0], axis=0) + jnp.take(ysort, slots[:, 1], axis=0)
    return x2d.reshape(batch, seq, d)
```

```python
import functools
import math

import jax
import jax.numpy as jnp
from jax import lax
from jax.experimental import pallas as pl
from jax.experimental.pallas import tpu as pltpu

EPS = 1e-6
N_HEADS = 8
QK_NOPE_DIM = 64
QK_ROPE_DIM = 32
QK_HEAD_DIM = QK_NOPE_DIM + QK_ROPE_DIM
V_HEAD_DIM = 64
LANES = 128
HEAD_PAD = LANES
MXU_DIM = 256
Q_LORA_RANK = 256
KV_LORA_RANK = 128
ROPE_THETA = 10000.0
S5_WIDTH = 512
S5_GROUP = 16
S5_GROUPS = S5_WIDTH // S5_GROUP
S5_STATE = 64
S5_CHUNK = 16
N_EXPERTS = 8
TOP_K = 2
ROUTER_PAD = LANES
NEG_BIG = -1e30
LOG2E = 1.4426950408889634

VMEM_LIMIT_BYTES = 56 * 1024 * 1024

F32 = jnp.float32
BF16 = jnp.bfloat16


def _dot(a, b):
    return jnp.dot(a, b, preferred_element_type=F32)


def _lane_tile(a, n):
    return jnp.concatenate([a] * n, axis=1)


def _rms_scale(x, ones_ref):
    ssq = _dot((x * x).astype(BF16), ones_ref[...])
    return lax.rsqrt(ssq * (1.0 / x.shape[1]) + EPS)


def _rms(x, g, ones_ref):
    return x * _lane_tile(_rms_scale(x, ones_ref), x.shape[1] // LANES) * g


def _params(*semantics):
    return pltpu.CompilerParams(dimension_semantics=semantics, vmem_limit_bytes=VMEM_LIMIT_BYTES)


def _full(shape):
    return pl.BlockSpec(shape, lambda *_: (0,) * len(shape))


def _inproj_kernel(x_ref, g_ref, onesd_ref, w1_ref, b1_ref, w2_ref, b2_ref, qng_ref, onesq_ref,
                   kvng_ref, oneskv_ref, wuq_ref, wuqr_ref, wk_ref, wv_ref, vones_ref, hones_ref,
                   qa_ref, qb_ref, ka_ref, kb_ref,
                   q_ref, k_ref, v_ref, u_ref, ga_ref, gb_ref):
    d = x_ref.shape[1]
    h = _rms(x_ref[...], g_ref[...], onesd_ref).astype(BF16)
    p1 = _dot(h, w1_ref[...]) + b1_ref[...]
    p2 = _dot(h, w2_ref[...]) + b2_ref[...]
    ga_ref[...] = jax.nn.sigmoid(p2[:, :d]).astype(BF16)
    gb_ref[...] = jax.nn.sigmoid(p2[:, d:]).astype(BF16)
    c0 = Q_LORA_RANK
    c1 = c0 + KV_LORA_RANK
    c2 = c1 + HEAD_PAD
    c3 = c2 + HEAD_PAD
    u_ref[...] = p1[:, c3:]
    qn = _rms(p1[:, :c0], qng_ref[...], onesq_ref).astype(BF16)
    kvn = _rms(p1[:, c0:c1], kvng_ref[...], oneskv_ref).astype(BF16)
    v_ref[...] = (_dot(kvn, wv_ref[...]) + vones_ref[...]).astype(BF16)

    def head_norm_rope(raw, rot, a_tab, b_tab):
        chunks = []
        for c in range(raw.shape[1] // MXU_DIM):
            blk = raw[:, c * MXU_DIM:(c + 1) * MXU_DIM]
            chunks.append(_dot((blk * blk).astype(BF16), hones_ref[...]))
        rs = lax.rsqrt(jnp.concatenate(chunks, axis=1) * (1.0 / QK_HEAD_DIM) + EPS)
        return rs * (raw * _lane_tile(a_tab, N_HEADS) + rot * _lane_tile(b_tab, N_HEADS))

    q_raw = _dot(qn, wuq_ref[...])
    q_rot = _dot(qn, wuqr_ref[...])
    q_ref[...] = head_norm_rope(q_raw, q_rot, qa_ref[...], qb_ref[...]).astype(BF16)
    k_raw = _dot(kvn, wk_ref[...]) + _lane_tile(p1[:, c1:c2], N_HEADS)
    k_rot = _lane_tile(p1[:, c2:c3], N_HEADS)
    k_ref[...] = head_norm_rope(k_raw, k_rot, ka_ref[...], kb_ref[...]).astype(BF16)


def _inproj(x2d, lp, seq, tm):
    t, d = x2d.shape
    nseq = seq // tm
    hp = N_HEADS * HEAD_PAD
    row = lambda i: (i, 0)
    pos = lambda i: (i % nseq, 0)
    consts = [lp[k] for k in ("ln_mix_g", "ones_d", "w1", "b1", "w2", "b2", "q_norm_g", "ones_q",
                              "kv_norm_g", "ones_kv", "wuq", "wuq_rot", "wk", "wv", "vones", "head_ones")]
    tabs = [lp[k] for k in ("qa", "qb", "ka", "kb")]
    in_specs = ([pl.BlockSpec((tm, d), row)] + [_full(c.shape) for c in consts]
                + [pl.BlockSpec((tm, HEAD_PAD), pos)] * 4)
    wide = lambda n, dt: (jax.ShapeDtypeStruct((t, n), dt), pl.BlockSpec((tm, n), row))
    outs = [wide(hp, BF16), wide(hp, BF16), wide(hp, BF16), wide(S5_WIDTH, F32), wide(d, BF16), wide(d, BF16)]
    return pl.pallas_call(
        _inproj_kernel, grid=(t // tm,), in_specs=in_specs,
        out_specs=tuple(o[1] for o in outs), out_shape=tuple(o[0] for o in outs),
        compiler_params=_params("parallel"), name="inproj",
    )(x2d, *consts, *tabs)


def _attn_kernel(q_ref, k_ref, v_ref, o_ref, m_scr, acc_scr):
    i = pl.program_id(1)
    j = pl.program_id(2)
    tq = q_ref.shape[0]
    tk = k_ref.shape[0]

    @pl.when(j == 0)
    def _():
        m_scr[...] = jnp.full(m_scr.shape, -jnp.inf, F32)
        acc_scr[...] = jnp.zeros(acc_scr.shape, F32)

    def step(masked):
        if masked:
            rows = lax.broadcasted_iota(jnp.int32, (tq, tk), 0)
            cols = lax.broadcasted_iota(jnp.int32, (tq, tk), 1)
            keep = cols <= rows
        for hd in range(N_HEADS):
            sl = slice(hd * HEAD_PAD, (hd + 1) * HEAD_PAD)
            s = lax.dot_general(q_ref[:, sl], k_ref[:, sl], (((1,), (1,)), ((), ())),
                                preferred_element_type=F32)
            if masked:
                s = jnp.where(keep, s, -jnp.inf)
            parts = [s[:, c * LANES:(c + 1) * LANES] for c in range(tk // LANES)]
            blk_max = functools.reduce(jnp.maximum, parts)
            m_prev = m_scr[hd]
            m_new = jnp.maximum(m_prev, jnp.max(blk_max, axis=-1, keepdims=True))
            alpha = jnp.exp2(m_prev - m_new)
            p = jnp.concatenate([jnp.exp2(part - m_new) for part in parts], axis=1).astype(BF16)
            acc_scr[hd] = alpha * acc_scr[hd] + _dot(p, v_ref[:, sl])
            m_scr[hd] = m_new

    @pl.when(j < i)
    def _():
        step(False)

    @pl.when(j == i)
    def _():
        step(True)
        lane = lax.broadcasted_iota(jnp.int32, (tq, HEAD_PAD), 1)
        for pr in range(N_HEADS // 2):
            a0 = acc_scr[2 * pr]
            a1 = acc_scr[2 * pr + 1]
            lo = a0 / pltpu.roll(a0, V_HEAD_DIM, 1)
            hi = pltpu.roll(a1, V_HEAD_DIM, 1) / a1
            o_ref[:, pr * HEAD_PAD:(pr + 1) * HEAD_PAD] = jnp.where(lane < V_HEAD_DIM, lo, hi).astype(BF16)


def _attention(q, k, v, batch, seq, tq):
    nq = seq // tq
    hp = N_HEADS * HEAD_PAD
    vw = N_HEADS * V_HEAD_DIM
    qmap = lambda b, i, j: (b * nq + i, 0)
    kmap = lambda b, i, j: (b * nq + jnp.minimum(j, i), 0)
    return pl.pallas_call(
        _attn_kernel, grid=(batch, nq, nq),
        in_specs=[pl.BlockSpec((tq, hp), qmap), pl.BlockSpec((tq, hp), kmap),
                  pl.BlockSpec((tq, hp), kmap)],
        out_specs=pl.BlockSpec((tq, vw), qmap),
        out_shape=jax.ShapeDtypeStruct((batch * seq, vw), BF16),
        scratch_shapes=[pltpu.VMEM((N_HEADS, tq, HEAD_PAD), F32), pltpu.VMEM((N_HEADS, tq, HEAD_PAD), F32)],
        compiler_params=_params("parallel", "parallel", "arbitrary"), name="attention",
    )(q, k, v)


def _s5_kernel(u_ref, m_ref, p_ref, q_ref, a_ref, y_ref, *, chunks_per_seq):
    ug = u_ref[0]
    rows = ug.shape[0]
    half = S5_STATE
    z = _dot(ug, p_ref[0])
    row_in_seq = lax.broadcasted_iota(jnp.int32, (rows, 2 * half), 0) & (chunks_per_seq - 1)
    a = a_ref[0]
    off = 1
    lvl = 0
    while off < chunks_per_seq:
        prev = pltpu.roll(z, off, 0)
        contrib = prev * a[2 * lvl:2 * lvl + 1] + pltpu.roll(prev, half, 1) * a[2 * lvl + 1:2 * lvl + 2]
        z = z + jnp.where(row_in_seq >= off, contrib, 0.0)
        off *= 2
        lvl += 1
    x_start = jnp.where(row_in_seq >= 1, pltpu.roll(z, 1, 0), 0.0)
    y_ref[0] = _dot(ug, m_ref[0]) + _dot(x_start.astype(BF16), q_ref[0])


def _s5_scan(ug, mats, chunks_per_seq):
    g, rows, w = ug.shape
    m, p, q, a = mats
    blk = lambda shape: pl.BlockSpec((1,) + shape, lambda i: (i, 0, 0))
    return pl.pallas_call(
        functools.partial(_s5_kernel, chunks_per_seq=chunks_per_seq), grid=(g,),
        in_specs=[blk((rows, w)), blk(m.shape[1:]), blk(p.shape[1:]), blk(q.shape[1:]), blk(a.shape[1:])],
        out_specs=blk((rows, w)), out_shape=jax.ShapeDtypeStruct((g, rows, w), F32),
        compiler_params=_params("parallel"), name="s5_scan",
    )(ug, m, p, q, a)


def _s5_matrices(lam_re, lam_im, log_step, b_re, b_im, c_re, c_im, chunks_per_seq):
    hi = lax.Precision.HIGHEST
    ln = S5_CHUNK
    step = jnp.exp(log_step)[:, None]
    ar = lam_re * step
    ai = lam_im * step
    lbr = jnp.exp(ar) * jnp.cos(ai)
    lbi = jnp.exp(ar) * jnp.sin(ai)
    den = lam_re * lam_re + lam_im * lam_im
    cr = ((lbr - 1.0) * lam_re + lbi * lam_im) / den
    ci = (lbi * lam_re - (lbr - 1.0) * lam_im) / den
    bbr = cr[..., None] * b_re - ci[..., None] * b_im
    bbi = cr[..., None] * b_im + ci[..., None] * b_re
    dd = jnp.arange(ln + 1, dtype=F32)[:, None, None]
    pr = jnp.exp(dd * ar) * jnp.cos(dd * ai)
    pi = jnp.exp(dd * ar) * jnp.sin(dd * ai)
    tr = pr[:ln, :, :, None] * bbr - pi[:ln, :, :, None] * bbi
    ti = pr[:ln, :, :, None] * bbi + pi[:ln, :, :, None] * bbr
    kd = (jnp.einsum("gcp,dgpe->dgce", c_re, tr, precision=hi)
          - jnp.einsum("gcp,dgpe->dgce", c_im, ti, precision=hi))
    lag = jnp.arange(ln)[None, :] - jnp.arange(ln)[:, None]
    toe = jnp.where((lag >= 0)[:, :, None, None, None], kd[jnp.clip(lag, 0, ln - 1)], 0.0)
    m = toe.transpose(2, 0, 4, 1, 3).reshape(S5_GROUPS, ln * S5_GROUP, ln * S5_GROUP)
    p_re = tr[::-1].transpose(1, 0, 3, 2).reshape(S5_GROUPS, ln * S5_GROUP, S5_STATE)
    p_im = ti[::-1].transpose(1, 0, 3, 2).reshape(S5_GROUPS, ln * S5_GROUP, S5_STATE)
    p = jnp.concatenate([p_re, p_im], axis=-1)
    q_re = c_re[None] * pr[1:, :, None, :] - c_im[None] * pi[1:, :, None, :]
    q_im = -(c_re[None] * pi[1:, :, None, :] + c_im[None] * pr[1:, :, None, :])
    q = jnp.concatenate([q_re.transpose(1, 3, 0, 2).reshape(S5_GROUPS, S5_STATE, ln * S5_GROUP),
                         q_im.transpose(1, 3, 0, 2).reshape(S5_GROUPS, S5_STATE, ln * S5_GROUP)], axis=1)
    levels = int(math.log2(chunks_per_seq))
    e = (ln * (2.0 ** jnp.arange(levels, dtype=F32)))[:, None, None]
    er = jnp.exp(e * ar) * jnp.cos(e * ai)
    ei = jnp.exp(e * ar) * jnp.sin(e * ai)
    a = jnp.stack([jnp.concatenate([er, er], -1), jnp.concatenate([-ei, ei], -1)], axis=1)
    a = a.transpose(2, 0, 1, 3).reshape(S5_GROUPS, 2 * levels, 2 * S5_STATE)
    return m.astype(BF16), p.astype(BF16), q.astype(BF16), a


def _split_bf16(a):
    hi = a.astype(BF16)
    return hi, (a - hi.astype(F32)).astype(BF16)


def _mix_kernel(*refs, routed):
    (x_ref, attn_ref, ys_ref, u_ref, ga_ref, gb_ref, d_ref, wglu_ref, bglu_ref, wpa_ref, wps_ref,
     wout_ref, lng_ref, onesd_ref) = refs[:14]
    if routed:
        rw_ref, rb_ref, xo_ref, h_ref, route_ref = refs[14:]
    else:
        xo_ref, h_ref = refs[14:]
    y = ys_ref[...] + d_ref[...] * u_ref[...]
    y = jax.nn.gelu(y, approximate=True)
    ssm = y * jax.nn.sigmoid(_dot(y.astype(BF16), wglu_ref[...]) + bglu_ref[...])
    pa = _dot(attn_ref[...], wpa_ref[...])
    ps = _dot(ssm.astype(BF16), wps_ref[...])
    mix = ga_ref[...].astype(F32) * pa + gb_ref[...].astype(F32) * ps
    xn = x_ref[...] + _dot(mix.astype(BF16), wout_ref[...])
    xo_ref[...] = xn
    h2 = _rms(xn, lng_ref[...], onesd_ref)
    if not routed:
        h_ref[...] = h2.astype(BF16)
        return
    h_ref[...] = h2
    h_hi, h_lo = _split_bf16(h2)
    r_hi, r_lo = _split_bf16(rw_ref[...])
    logits = _dot(h_hi, r_hi) + _dot(h_lo, r_hi) + _dot(h_hi, r_lo) + rb_ref[...]
    lane = lax.broadcasted_iota(jnp.int32, logits.shape, 1).astype(F32)
    m1 = jnp.max(logits, axis=-1, keepdims=True)
    i1 = jnp.min(jnp.where(logits == m1, lane, float(ROUTER_PAD)), axis=-1, keepdims=True)
    rest = jnp.where(lane == i1, NEG_BIG, logits)
    m2 = jnp.max(rest, axis=-1, keepdims=True)
    i2 = jnp.min(jnp.where(rest == m2, lane, float(ROUTER_PAD)), axis=-1, keepdims=True)
    e2 = jnp.exp(m2 - m1)
    w1 = 1.0 / (1.0 + e2)
    route_ref[...] = jnp.where(lane == 0.0, i1, jnp.where(lane == 1.0, i2, jnp.where(
        lane == 2.0, w1, jnp.where(lane == 3.0, e2 * w1, 0.0))))


def _mix(x2d, attn, ys, u, ga, gb, lp, tm, router=None):
    t, d = x2d.shape
    row = lambda i: (i, 0)
    routed = router is not None
    consts = [lp[k] for k in ("s5_d", "w_glu", "b_glu", "w_proj_attn", "w_proj_ssm", "w_out",
                              "ln_ffn_g", "ones_d")]
    if routed:
        consts += list(router)
    acts = [x2d, attn, ys, u, ga, gb]
    in_specs = [pl.BlockSpec((tm, a.shape[1]), row) for a in acts] + [_full(c.shape) for c in consts]
    if routed:
        out_shape = [jax.ShapeDtypeStruct((t, d), F32), jax.ShapeDtypeStruct((t, d), F32),
                     jax.ShapeDtypeStruct((t, ROUTER_PAD), F32)]
    else:
        out_shape = [jax.ShapeDtypeStruct((t, d), F32), jax.ShapeDtypeStruct((t, d), BF16)]
    out_specs = [pl.BlockSpec((tm, o.shape[1]), row) for o in out_shape]
    return pl.pallas_call(
        functools.partial(_mix_kernel, routed=routed), grid=(t // tm,), in_specs=in_specs,
        out_specs=tuple(out_specs), out_shape=tuple(out_shape),
        compiler_params=_params("parallel"), name="mix_routed" if routed else "mix",
    )(*acts, *consts)


def _ffn_kernel(te_ref, nu_ref, *refs, routed):
    if routed:
        x_ref, wg_ref, wu_ref, wd_ref, o_ref, acc_scr, x_scr = refs
    else:
        x_ref, wg_ref, wu_ref, wd_ref, res_ref, o_ref, acc_scr = refs
    i = pl.program_id(0)
    j = pl.program_id(1)
    last = pl.num_programs(1) - 1
    used = i < nu_ref[0]

    if routed:
        @pl.when(jnp.logical_and(used, j == 0))
        def _():
            x_scr[...] = x_ref[...].astype(BF16)

    @pl.when(used)
    def _():
        x = x_scr[...] if routed else x_ref[...]
        g = _dot(x, wg_ref[0])
        a = (g * jax.nn.sigmoid(g) * _dot(x, wu_ref[0])).astype(BF16)
        part = _dot(a, wd_ref[0])

        @pl.when(j == 0)
        def _():
            acc_scr[...] = part

        @pl.when(j > 0)
        def _():
            acc_scr[...] += part

    @pl.when(jnp.logical_and(used, j == last))
    def _():
        if routed:
            o_ref[...] = acc_scr[...]
        else:
            o_ref[...] = acc_scr[...] + res_ref[...]

    @pl.when(jnp.logical_and(jnp.logical_not(used), j == last))
    def _():
        o_ref[...] = jnp.zeros(o_ref.shape, o_ref.dtype)


def _grouped_ffn(tile_expert, n_used, xs, wg, wu, wd, res, tm, tf):
    routed = res is None
    n = xs.shape[0]
    d = wg.shape[1]
    nff = wg.shape[2] // tf

    def jeff(i, j, nu):
        return jnp.where(i < nu[0], j, nff - 1)

    row = lambda i, j, te, nu: (i, 0)
    in_specs = [
        pl.BlockSpec((tm, xs.shape[1]), row),
        pl.BlockSpec((1, d, tf), lambda i, j, te, nu: (te[i], 0, jeff(i, j, nu))),
        pl.BlockSpec((1, d, tf), lambda i, j, te, nu: (te[i], 0, jeff(i, j, nu))),
        pl.BlockSpec((1, tf, d), lambda i, j, te, nu: (te[i], jeff(i, j, nu), 0)),
    ]
    args = [xs, wg, wu, wd]
    scratch = [pltpu.VMEM((tm, d), F32)]
    if routed:
        scratch.append(pltpu.VMEM((tm, d), BF16))
    else:
        in_specs.append(pl.BlockSpec((tm, d), row))
        args.append(res)
    grid_spec = pltpu.PrefetchScalarGridSpec(
        num_scalar_prefetch=2, grid=(n // tm, nff), in_specs=in_specs,
        out_specs=pl.BlockSpec((tm, d), row), scratch_shapes=scratch)
    return pl.pallas_call(
        functools.partial(_ffn_kernel, routed=routed), grid_spec=grid_spec,
        out_shape=jax.ShapeDtypeStruct((n, d), F32),
        compiler_params=_params("parallel", "arbitrary"),
        name="ffn_experts" if routed else "ffn_dense",
    )(tile_expert, n_used, *args)


def _combine_kernel(x_ref, y0_ref, y1_ref, route_ref, o_ref):
    r = route_ref[...]
    w0 = r[:, 2:3]
    w1 = r[:, 3:4]
    o_ref[...] = x_ref[...] + w0 * y0_ref[...] + w1 * y1_ref[...]


def _combine(x2d, y0, y1, route, tm):
    t, d = x2d.shape
    row = lambda i: (i, 0)
    return pl.pallas_call(
        _combine_kernel, grid=(t // tm,),
        in_specs=[pl.BlockSpec((tm, d), row), pl.BlockSpec((tm, d), row),
                  pl.BlockSpec((tm, d), row), pl.BlockSpec((tm, ROUTER_PAD), row)],
        out_specs=pl.BlockSpec((tm, d), row), out_shape=jax.ShapeDtypeStruct((t, d), F32),
        compiler_params=_params("parallel"), name="combine",
    )(x2d, y0, y1, route)


def _rope_tables(seq):
    half = QK_ROPE_DIM // 2
    pos = jnp.arange(seq, dtype=F32)
    inv = jnp.power(ROPE_THETA, -jnp.arange(0, QK_ROPE_DIM, 2, dtype=F32) / QK_ROPE_DIM)
    ang = pos[:, None] * inv[None, :]
    cos, sin = jnp.cos(ang), jnp.sin(ang)
    tail = jnp.zeros((seq, HEAD_PAD - QK_HEAD_DIM), F32)
    c = jnp.concatenate([jnp.ones((seq, QK_NOPE_DIM), F32), cos, cos, tail], axis=1)
    s = jnp.concatenate([jnp.zeros((seq, QK_NOPE_DIM), F32), -sin, sin, tail], axis=1)
    return c, s


def _rope_partner(a):
    half = QK_ROPE_DIM // 2
    lead = a.shape[:-1]
    a = a.reshape(lead + (-1, HEAD_PAD))
    x1 = a[..., QK_NOPE_DIM:QK_NOPE_DIM + half]
    x2 = a[..., QK_NOPE_DIM + half:QK_HEAD_DIM]
    out = jnp.concatenate([jnp.zeros_like(a[..., :QK_NOPE_DIM]), x2, x1,
                           jnp.zeros_like(a[..., QK_HEAD_DIM:])], axis=-1)
    return out.reshape(lead + (-1,))


def _pad_heads(w, per_head, place_at=0):
    k = w.shape[0]
    w = w.reshape(k, N_HEADS, per_head)
    w = jnp.pad(w, ((0, 0), (0, 0), (place_at, HEAD_PAD - per_head - place_at)))
    return w.reshape(k, N_HEADS * HEAD_PAD)


def _layer_params(l, p, rope):
    d = p["w_in"].shape[1]
    w_in, b_in = p["w_in"][l], p["b_in"][l]
    o = [0, Q_LORA_RANK, Q_LORA_RANK + KV_LORA_RANK, Q_LORA_RANK + KV_LORA_RANK + QK_ROPE_DIM]
    o.append(o[3] + S5_WIDTH)
    o.append(o[4] + d)
    pad_rope = lambda a: jnp.pad(a, ((0, 0), (QK_NOPE_DIM, HEAD_PAD - QK_HEAD_DIM)))
    w_kr, b_kr = pad_rope(w_in[:, o[2]:o[3]]), pad_rope(b_in[None, o[2]:o[3]])
    w1 = jnp.concatenate([w_in[:, o[0]:o[2]], w_kr, _rope_partner(w_kr), w_in[:, o[3]:o[4]]], axis=1)
    b1 = jnp.concatenate([b_in[None, o[0]:o[2]], b_kr, _rope_partner(b_kr), b_in[None, o[3]:o[4]]], axis=1)
    w_ukv = p["w_ukv"][l].reshape(KV_LORA_RANK, N_HEADS, QK_NOPE_DIM + V_HEAD_DIM)
    wuq = _pad_heads(p["w_uq"][l], QK_HEAD_DIM)
    pad_g = lambda g: jnp.pad(g[None, :], ((0, 0), (0, HEAD_PAD - QK_HEAD_DIM)))
    qg, kg = pad_g(p["q_head_g"][l]), pad_g(p["k_head_g"][l])
    cos, sin = rope
    qscale = QK_HEAD_DIM ** -0.5 * LOG2E
    head_ones = jnp.kron(jnp.eye(MXU_DIM // HEAD_PAD, dtype=F32), jnp.ones((HEAD_PAD, HEAD_PAD), F32))
    return {
        "ln_mix_g": p["ln_mix_g"][l][None], "ones_d": jnp.ones((d, LANES), BF16),
        "w1": w1.astype(BF16), "b1": b1, "w2": w_in[:, o[4]:].astype(BF16), "b2": b_in[None, o[4]:],
        "q_norm_g": p["q_norm_g"][l][None], "ones_q": jnp.ones((Q_LORA_RANK, LANES), BF16),
        "kv_norm_g": p["kv_norm_g"][l][None], "ones_kv": jnp.ones((KV_LORA_RANK, LANES), BF16),
        "wuq": wuq.astype(BF16), "wuq_rot": _rope_partner(wuq).astype(BF16),
        "wk": _pad_heads(w_ukv[:, :, :QK_NOPE_DIM].reshape(KV_LORA_RANK, -1), QK_NOPE_DIM).astype(BF16),
        "wv": _pad_heads(w_ukv[:, :, QK_NOPE_DIM:].reshape(KV_LORA_RANK, -1), V_HEAD_DIM).astype(BF16),
        "vones": _pad_heads(jnp.ones((1, N_HEADS * (HEAD_PAD - V_HEAD_DIM)), F32),
                            HEAD_PAD - V_HEAD_DIM, place_at=V_HEAD_DIM),
        "head_ones": head_ones.astype(BF16),
        "qa": qg * cos * qscale, "qb": _rope_partner(qg) * sin * qscale,
        "ka": kg * cos, "kb": _rope_partner(kg) * sin,
        "s5_d": p["s5_d"][l][None], "w_glu": p["w_glu"][l].astype(BF16), "b_glu": p["b_glu"][l][None],
        "w_proj_attn": p["w_proj_attn"][l].astype(BF16), "w_proj_ssm": p["w_proj_ssm"][l].astype(BF16),
        "w_out": p["w_out"][l].astype(BF16), "ln_ffn_g": p["ln_ffn_g"][l][None],
    }


def _routing(route, tm):
    t = route.shape[0]
    e = route[:, :TOP_K].astype(jnp.int32)
    seli = jnp.sum((e[:, :, None] == jnp.arange(N_EXPERTS)[None, None, :]).astype(jnp.int32), axis=1)
    cnt = jnp.sum(seli, axis=0)
    padded = ((cnt + tm - 1) // tm) * tm
    ends = jnp.cumsum(padded)
    pos_te = (ends - padded)[None, :] + jnp.cumsum(seli, axis=0) - seli
    slots = jnp.take_along_axis(pos_te, e, axis=1)
    n_slots = TOP_K * t + N_EXPERTS * tm
    tok = jnp.broadcast_to(jnp.arange(t, dtype=jnp.int32)[:, None], (t, TOP_K))
    tok_of_slot = jnp.zeros((n_slots,), jnp.int32).at[slots.reshape(-1)].set(
        tok.reshape(-1), unique_indices=True)
    tile_start = jnp.arange(n_slots // tm, dtype=jnp.int32) * tm
    tile_expert = jnp.sum((tile_start[:, None] >= ends[None, :]).astype(jnp.int32), axis=1)
    tile_expert = jnp.minimum(tile_expert, N_EXPERTS - 1)
    n_used = (ends[-1] // tm).astype(jnp.int32)[None]
    tile_expert = jnp.where(tile_start < ends[-1], tile_expert, tile_expert[jnp.maximum(n_used[0] - 1, 0)])
    return tok_of_slot, tile_expert.astype(jnp.int32), n_used, slots


def kernel(x, ln_mix_g, w_in, b_in, q_norm_g, w_uq, kv_norm_g, w_ukv, q_head_g, k_head_g,
           s5_lam_re, s5_lam_im, s5_log_step, s5_b_re, s5_b_im, s5_c_re, s5_c_im, s5_d,
           w_glu, b_glu, w_proj_attn, w_proj_ssm, w_out, ln_ffn_g,
           ffn_w_gate, ffn_w_up, ffn_w_down, router_w, router_b,
           moe_w_gate, moe_w_up, moe_w_down):
    batch, seq, d = x.shape
    depth = w_in.shape[0]
    t = batch * seq
    tm = min(512, seq)
    tq = min(512, seq)
    tm_ffn = min(1024, t)
    chunks_per_seq = seq // S5_CHUNK
    assert chunks_per_seq & (chunks_per_seq - 1) == 0 and seq % tm == 0 and t % tm_ffn == 0
    p = dict(ln_mix_g=ln_mix_g, w_in=w_in, b_in=b_in, q_norm_g=q_norm_g, w_uq=w_uq,
             kv_norm_g=kv_norm_g, w_ukv=w_ukv, q_head_g=q_head_g, k_head_g=k_head_g, s5_d=s5_d,
             w_glu=w_glu, b_glu=b_glu, w_proj_attn=w_proj_attn, w_proj_ssm=w_proj_ssm,
             w_out=w_out, ln_ffn_g=ln_ffn_g)
    rope = _rope_tables(seq)
    x2d = x.reshape(t, d)
    for l in range(depth):
        lp = _layer_params(l, p, rope)
        q, k, v, u, ga, gb = _inproj(x2d, lp, seq, tm)
        attn = _attention(q, k, v, batch, seq, tq)
        mats = _s5_matrices(s5_lam_re[l], s5_lam_im[l], s5_log_step[l], s5_b_re[l], s5_b_im[l],
                            s5_c_re[l], s5_c_im[l], chunks_per_seq)
        ug = (u.astype(BF16).reshape(t // S5_CHUNK, S5_CHUNK, S5_GROUPS, S5_GROUP)
              .transpose(2, 0, 1, 3).reshape(S5_GROUPS, t // S5_CHUNK, S5_CHUNK * S5_GROUP))
        yg = _s5_scan(ug, mats, chunks_per_seq)
        ys = (yg.reshape(S5_GROUPS, t // S5_CHUNK, S5_CHUNK, S5_GROUP)
              .transpose(1, 2, 0, 3).reshape(t, S5_WIDTH))
        j = l // 2
        if l % 2 == 0:
            x2d, h2 = _mix(x2d, attn, ys, u, ga, gb, lp, tm)
            n_tiles = t // tm_ffn
            x2d = _grouped_ffn(
                jnp.zeros((n_tiles,), jnp.int32), jnp.full((1,), n_tiles, jnp.int32), h2,
                ffn_w_gate[j][None].astype(BF16), ffn_w_up[j][None].astype(BF16),
                ffn_w_down[j][None].astype(BF16), x2d, tm_ffn, 256)
        else:
            rw = jnp.pad(router_w[j], ((0, 0), (0, ROUTER_PAD - N_EXPERTS)))
            rb = jnp.pad(router_b[j][None], ((0, 0), (0, ROUTER_PAD - N_EXPERTS)), constant_values=NEG_BIG)
            x2d, h2, route = _mix(x2d, attn, ys, u, ga, gb, lp, tm, router=(rw, rb))
            tok_of_slot, tile_expert, n_used, slots = _routing(route, tm_ffn)
            xs = jnp.take(h2, tok_of_slot, axis=0)
            ysort = _grouped_ffn(tile_expert, n_used, xs, moe_w_gate[j].astype(BF16),
                                 moe_w_up[j].astype(BF16), moe_w_down[j].astype(BF16), None, tm_ffn, 512)
            x2d = _combine(x2d, jnp.take(ysort, slots[:, 0], axis=0),
                           jnp.take(ysort, slots[:, 1], axis=0), route, tm)
    return x2d.reshape(batch, seq, d)
```

```python
import functools
import math

import jax
import jax.numpy as jnp
from jax import lax
from jax.experimental import pallas as pl
from jax.experimental.pallas import tpu as pltpu

EPS = 1e-6
N_HEADS = 8
QK_NOPE_DIM = 64
QK_ROPE_DIM = 32
QK_HEAD_DIM = QK_NOPE_DIM + QK_ROPE_DIM
V_HEAD_DIM = 64
LANES = 128
HEAD_PAD = LANES
MXU_DIM = 256
Q_LORA_RANK = 256
KV_LORA_RANK = 128
ROPE_THETA = 10000.0
S5_WIDTH = 512
S5_GROUP = 16
S5_GROUPS = S5_WIDTH // S5_GROUP
S5_STATE = 64
S5_CHUNK = 16
N_EXPERTS = 8
TOP_K = 2
ROUTER_PAD = LANES
NEG_BIG = -1e30
LOG2E = 1.4426950408889634

VMEM_LIMIT_BYTES = 56 * 1024 * 1024

F32 = jnp.float32
BF16 = jnp.bfloat16


def _dot(a, b):
    return jnp.dot(a, b, preferred_element_type=F32)


def _lane_tile(a, n):
    return jnp.concatenate([a] * n, axis=1)


def _rms_scale(x, ones_ref):
    ssq = _dot((x * x).astype(BF16), ones_ref[...])
    return lax.rsqrt(ssq * (1.0 / x.shape[1]) + EPS)


def _rms(x, g, ones_ref):
    return x * _lane_tile(_rms_scale(x, ones_ref), x.shape[1] // LANES) * g


def _params(*semantics):
    return pltpu.CompilerParams(dimension_semantics=semantics, vmem_limit_bytes=VMEM_LIMIT_BYTES)


def _full(shape):
    return pl.BlockSpec(shape, lambda *_: (0,) * len(shape))


def _inproj_kernel(x_ref, g_ref, onesd_ref, w1_ref, b1_ref, w2_ref, b2_ref, qng_ref, onesq_ref,
                   kvng_ref, oneskv_ref, wuq_ref, wuqr_ref, wk_ref, wv_ref, vones_ref, hones_ref,
                   qa_ref, qb_ref, ka_ref, kb_ref,
                   q_ref, k_ref, v_ref, u_ref, ga_ref, gb_ref):
    d = x_ref.shape[1]
    h = _rms(x_ref[...], g_ref[...], onesd_ref).astype(BF16)
    p1 = _dot(h, w1_ref[...]) + b1_ref[...]
    p2 = _dot(h, w2_ref[...]) + b2_ref[...]
    ga_ref[...] = jax.nn.sigmoid(p2[:, :d]).astype(BF16)
    gb_ref[...] = jax.nn.sigmoid(p2[:, d:]).astype(BF16)
    c0 = Q_LORA_RANK
    c1 = c0 + KV_LORA_RANK
    c2 = c1 + HEAD_PAD
    c3 = c2 + HEAD_PAD
    u_ref[...] = p1[:, c3:]
    qn = _rms(p1[:, :c0], qng_ref[...], onesq_ref).astype(BF16)
    kvn = _rms(p1[:, c0:c1], kvng_ref[...], oneskv_ref).astype(BF16)
    v_ref[...] = (_dot(kvn, wv_ref[...]) + vones_ref[...]).astype(BF16)

    def head_norm_rope(raw, rot, a_tab, b_tab):
        chunks = []
        for c in range(raw.shape[1] // MXU_DIM):
            blk = raw[:, c * MXU_DIM:(c + 1) * MXU_DIM]
            chunks.append(_dot((blk * blk).astype(BF16), hones_ref[...]))
        rs = lax.rsqrt(jnp.concatenate(chunks, axis=1) * (1.0 / QK_HEAD_DIM) + EPS)
        return rs * (raw * _lane_tile(a_tab, N_HEADS) + rot * _lane_tile(b_tab, N_HEADS))

    q_raw = _dot(qn, wuq_ref[...])
    q_rot = _dot(qn, wuqr_ref[...])
    q_ref[...] = head_norm_rope(q_raw, q_rot, qa_ref[...], qb_ref[...]).astype(BF16)
    k_raw = _dot(kvn, wk_ref[...]) + _lane_tile(p1[:, c1:c2], N_HEADS)
    k_rot = _lane_tile(p1[:, c2:c3], N_HEADS)
    k_ref[...] = head_norm_rope(k_raw, k_rot, ka_ref[...], kb_ref[...]).astype(BF16)


def _inproj(x2d, lp, seq, tm):
    t, d = x2d.shape
    nseq = seq // tm
    hp = N_HEADS * HEAD_PAD
    row = lambda i: (i, 0)
    pos = lambda i: (i % nseq, 0)
    consts = [lp[k] for k in ("ln_mix_g", "ones_d", "w1", "b1", "w2", "b2", "q_norm_g", "ones_q",
                              "kv_norm_g", "ones_kv", "wuq", "wuq_rot", "wk", "wv", "vones", "head_ones")]
    tabs = [lp[k] for k in ("qa", "qb", "ka", "kb")]
    in_specs = ([pl.BlockSpec((tm, d), row)] + [_full(c.shape) for c in consts]
                + [pl.BlockSpec((tm, HEAD_PAD), pos)] * 4)
    wide = lambda n, dt: (jax.ShapeDtypeStruct((t, n), dt), pl.BlockSpec((tm, n), row))
    outs = [wide(hp, BF16), wide(hp, BF16), wide(hp, BF16), wide(S5_WIDTH, F32), wide(d, BF16), wide(d, BF16)]
    return pl.pallas_call(
        _inproj_kernel, grid=(t // tm,), in_specs=in_specs,
        out_specs=tuple(o[1] for o in outs), out_shape=tuple(o[0] for o in outs),
        compiler_params=_params("parallel"), name="inproj",
    )(x2d, *consts, *tabs)


def _attn_kernel(q_ref, k_ref, v_ref, o_ref, m_scr, acc_scr):
    i = pl.program_id(1)
    j = pl.program_id(2)
    tq = q_ref.shape[0]
    tk = k_ref.shape[0]

    @pl.when(j == 0)
    def _():
        m_scr[...] = jnp.full(m_scr.shape, -jnp.inf, F32)
        acc_scr[...] = jnp.zeros(acc_scr.shape, F32)

    def step(masked):
        if masked:
            rows = lax.broadcasted_iota(jnp.int32, (tq, tk), 0)
            cols = lax.broadcasted_iota(jnp.int32, (tq, tk), 1)
            keep = cols <= rows
        for hd in range(N_HEADS):
            sl = slice(hd * HEAD_PAD, (hd + 1) * HEAD_PAD)
            s = lax.dot_general(q_ref[:, sl], k_ref[:, sl], (((1,), (1,)), ((), ())),
                                preferred_element_type=F32)
            if masked:
                s = jnp.where(keep, s, -jnp.inf)
            parts = [s[:, c * LANES:(c + 1) * LANES] for c in range(tk // LANES)]
            blk_max = functools.reduce(jnp.maximum, parts)
            m_prev = m_scr[hd]
            m_new = jnp.maximum(m_prev, jnp.max(blk_max, axis=-1, keepdims=True))
            alpha = jnp.exp2(m_prev - m_new)
            p = jnp.concatenate([jnp.exp2(part - m_new) for part in parts], axis=1).astype(BF16)
            acc_scr[hd] = alpha * acc_scr[hd] + _dot(p, v_ref[:, sl])
            m_scr[hd] = m_new

    @pl.when(j < i)
    def _():
        step(False)

    @pl.when(j == i)
    def _():
        step(True)
        lane = lax.broadcasted_iota(jnp.int32, (tq, HEAD_PAD), 1)
        for pr in range(N_HEADS // 2):
            a0 = acc_scr[2 * pr]
            a1 = acc_scr[2 * pr + 1]
            lo = a0 / pltpu.roll(a0, V_HEAD_DIM, 1)
            hi = pltpu.roll(a1, V_HEAD_DIM, 1) / a1
            o_ref[:, pr * HEAD_PAD:(pr + 1) * HEAD_PAD] = jnp.where(lane < V_HEAD_DIM, lo, hi).astype(BF16)


def _attention(q, k, v, batch, seq, tq):
    nq = seq // tq
    hp = N_HEADS * HEAD_PAD
    vw = N_HEADS * V_HEAD_DIM
    qmap = lambda b, i, j: (b * nq + i, 0)
    kmap = lambda b, i, j: (b * nq + jnp.minimum(j, i), 0)
    return pl.pallas_call(
        _attn_kernel, grid=(batch, nq, nq),
        in_specs=[pl.BlockSpec((tq, hp), qmap), pl.BlockSpec((tq, hp), kmap),
                  pl.BlockSpec((tq, hp), kmap)],
        out_specs=pl.BlockSpec((tq, vw), qmap),
        out_shape=jax.ShapeDtypeStruct((batch * seq, vw), BF16),
        scratch_shapes=[pltpu.VMEM((N_HEADS, tq, HEAD_PAD), F32), pltpu.VMEM((N_HEADS, tq, HEAD_PAD), F32)],
        compiler_params=_params("parallel", "parallel", "arbitrary"), name="attention",
    )(q, k, v)


def _s5_kernel(u_ref, m_ref, p_ref, q_ref, a_ref, y_ref, *, chunks_per_seq):
    ug = u_ref[0]
    rows = ug.shape[0]
    half = S5_STATE
    z = _dot(ug, p_ref[0])
    row_in_seq = lax.broadcasted_iota(jnp.int32, (rows, 2 * half), 0) & (chunks_per_seq - 1)
    a = a_ref[0]
    off = 1
    lvl = 0
    while off < chunks_per_seq:
        prev = pltpu.roll(z, off, 0)
        contrib = prev * a[2 * lvl:2 * lvl + 1] + pltpu.roll(prev, half, 1) * a[2 * lvl + 1:2 * lvl + 2]
        z = z + jnp.where(row_in_seq >= off, contrib, 0.0)
        off *= 2
        lvl += 1
    x_start = jnp.where(row_in_seq >= 1, pltpu.roll(z, 1, 0), 0.0)
    y_ref[0] = (_dot(ug, m_ref[0]) + _dot(x_start.astype(BF16), q_ref[0])).astype(y_ref.dtype)


def _s5_scan(ug, mats, chunks_per_seq):
    g, rows, w = ug.shape
    m, p, q, a = mats
    blk = lambda shape: pl.BlockSpec((1,) + shape, lambda i: (i, 0, 0))
    return pl.pallas_call(
        functools.partial(_s5_kernel, chunks_per_seq=chunks_per_seq), grid=(g,),
        in_specs=[blk((rows, w)), blk(m.shape[1:]), blk(p.shape[1:]), blk(q.shape[1:]), blk(a.shape[1:])],
        out_specs=blk((rows, w)), out_shape=jax.ShapeDtypeStruct((g, rows, w), BF16),
        compiler_params=_params("parallel"), name="s5_scan",
    )(ug, m, p, q, a)


def _s5_matrices(lam_re, lam_im, log_step, b_re, b_im, c_re, c_im, chunks_per_seq):
    hi = lax.Precision.HIGHEST
    ln = S5_CHUNK
    step = jnp.exp(log_step)[:, None]
    ar = lam_re * step
    ai = lam_im * step
    lbr = jnp.exp(ar) * jnp.cos(ai)
    lbi = jnp.exp(ar) * jnp.sin(ai)
    den = lam_re * lam_re + lam_im * lam_im
    cr = ((lbr - 1.0) * lam_re + lbi * lam_im) / den
    ci = (lbi * lam_re - (lbr - 1.0) * lam_im) / den
    bbr = cr[..., None] * b_re - ci[..., None] * b_im
    bbi = cr[..., None] * b_im + ci[..., None] * b_re
    dd = jnp.arange(ln + 1, dtype=F32)[:, None, None]
    pr = jnp.exp(dd * ar) * jnp.cos(dd * ai)
    pi = jnp.exp(dd * ar) * jnp.sin(dd * ai)
    tr = pr[:ln, :, :, None] * bbr - pi[:ln, :, :, None] * bbi
    ti = pr[:ln, :, :, None] * bbi + pi[:ln, :, :, None] * bbr
    kd = (jnp.einsum("gcp,dgpe->dgce", c_re, tr, precision=hi)
          - jnp.einsum("gcp,dgpe->dgce", c_im, ti, precision=hi))
    lag = jnp.arange(ln)[None, :] - jnp.arange(ln)[:, None]
    toe = jnp.where((lag >= 0)[:, :, None, None, None], kd[jnp.clip(lag, 0, ln - 1)], 0.0)
    m = toe.transpose(2, 0, 4, 1, 3).reshape(S5_GROUPS, ln * S5_GROUP, ln * S5_GROUP)
    p_re = tr[::-1].transpose(1, 0, 3, 2).reshape(S5_GROUPS, ln * S5_GROUP, S5_STATE)
    p_im = ti[::-1].transpose(1, 0, 3, 2).reshape(S5_GROUPS, ln * S5_GROUP, S5_STATE)
    p = jnp.concatenate([p_re, p_im], axis=-1)
    q_re = c_re[None] * pr[1:, :, None, :] - c_im[None] * pi[1:, :, None, :]
    q_im = -(c_re[None] * pi[1:, :, None, :] + c_im[None] * pr[1:, :, None, :])
    q = jnp.concatenate([q_re.transpose(1, 3, 0, 2).reshape(S5_GROUPS, S5_STATE, ln * S5_GROUP),
                         q_im.transpose(1, 3, 0, 2).reshape(S5_GROUPS, S5_STATE, ln * S5_GROUP)], axis=1)
    levels = int(math.log2(chunks_per_seq))
    e = (ln * (2.0 ** jnp.arange(levels, dtype=F32)))[:, None, None]
    er = jnp.exp(e * ar) * jnp.cos(e * ai)
    ei = jnp.exp(e * ar) * jnp.sin(e * ai)
    a = jnp.stack([jnp.concatenate([er, er], -1), jnp.concatenate([-ei, ei], -1)], axis=1)
    a = a.transpose(2, 0, 1, 3).reshape(S5_GROUPS, 2 * levels, 2 * S5_STATE)
    return m.astype(BF16), p.astype(BF16), q.astype(BF16), a


def _split_bf16(a):
    hi = a.astype(BF16)
    return hi, (a - hi.astype(F32)).astype(BF16)


def _mix_kernel(*refs, routed):
    (x_ref, attn_ref, ys_ref, u_ref, ga_ref, gb_ref, d_ref, wglu_ref, bglu_ref, wpa_ref, wps_ref,
     wout_ref, lng_ref, onesd_ref) = refs[:14]
    if routed:
        rw_ref, rb_ref, xo_ref, h_ref, route_ref = refs[14:]
    else:
        xo_ref, h_ref = refs[14:]
    y = ys_ref[...] + d_ref[...] * u_ref[...]
    y = jax.nn.gelu(y, approximate=True)
    ssm = y * jax.nn.sigmoid(_dot(y.astype(BF16), wglu_ref[...]) + bglu_ref[...])
    pa = _dot(attn_ref[...], wpa_ref[...])
    ps = _dot(ssm.astype(BF16), wps_ref[...])
    mix = ga_ref[...].astype(F32) * pa + gb_ref[...].astype(F32) * ps
    xn = x_ref[...] + _dot(mix.astype(BF16), wout_ref[...])
    xo_ref[...] = xn
    h2 = _rms(xn, lng_ref[...], onesd_ref)
    if not routed:
        h_ref[...] = h2.astype(BF16)
        return
    h_ref[...] = h2
    h_hi, h_lo = _split_bf16(h2)
    r_hi, r_lo = _split_bf16(rw_ref[...])
    logits = _dot(h_hi, r_hi) + _dot(h_lo, r_hi) + _dot(h_hi, r_lo) + rb_ref[...]
    lane = lax.broadcasted_iota(jnp.int32, logits.shape, 1).astype(F32)
    m1 = jnp.max(logits, axis=-1, keepdims=True)
    i1 = jnp.min(jnp.where(logits == m1, lane, float(ROUTER_PAD)), axis=-1, keepdims=True)
    rest = jnp.where(lane == i1, NEG_BIG, logits)
    m2 = jnp.max(rest, axis=-1, keepdims=True)
    i2 = jnp.min(jnp.where(rest == m2, lane, float(ROUTER_PAD)), axis=-1, keepdims=True)
    e2 = jnp.exp(m2 - m1)
    w1 = 1.0 / (1.0 + e2)
    route_ref[...] = jnp.where(lane == 0.0, i1, jnp.where(lane == 1.0, i2, jnp.where(
        lane == 2.0, w1, jnp.where(lane == 3.0, e2 * w1, 0.0))))


def _mix(x2d, attn, ys, u, ga, gb, lp, tm, router=None):
    t, d = x2d.shape
    row = lambda i: (i, 0)
    routed = router is not None
    consts = [lp[k] for k in ("s5_d", "w_glu", "b_glu", "w_proj_attn", "w_proj_ssm", "w_out",
                              "ln_ffn_g", "ones_d")]
    if routed:
        consts += list(router)
    acts = [x2d, attn, ys, u, ga, gb]
    in_specs = [pl.BlockSpec((tm, a.shape[1]), row) for a in acts] + [_full(c.shape) for c in consts]
    if routed:
        out_shape = [jax.ShapeDtypeStruct((t, d), F32), jax.ShapeDtypeStruct((t, d), F32),
                     jax.ShapeDtypeStruct((t, ROUTER_PAD), F32)]
    else:
        out_shape = [jax.ShapeDtypeStruct((t, d), F32), jax.ShapeDtypeStruct((t, d), BF16)]
    out_specs = [pl.BlockSpec((tm, o.shape[1]), row) for o in out_shape]
    return pl.pallas_call(
        functools.partial(_mix_kernel, routed=routed), grid=(t // tm,), in_specs=in_specs,
        out_specs=tuple(out_specs), out_shape=tuple(out_shape),
        compiler_params=_params("parallel"), name="mix_routed" if routed else "mix",
    )(*acts, *consts)


def _ffn_kernel(te_ref, nu_ref, *refs, routed):
    if routed:
        x_ref, wg_ref, wu_ref, wd_ref, o_ref, acc_scr, x_scr = refs
    else:
        x_ref, wg_ref, wu_ref, wd_ref, res_ref, o_ref, acc_scr = refs
    i = pl.program_id(0)
    j = pl.program_id(1)
    last = pl.num_programs(1) - 1
    used = i < nu_ref[0]

    if routed:
        @pl.when(jnp.logical_and(used, j == 0))
        def _():
            x_scr[...] = x_ref[...].astype(BF16)

    @pl.when(used)
    def _():
        x = x_scr[...] if routed else x_ref[...]
        g = _dot(x, wg_ref[0].astype(BF16))
        a = (g * jax.nn.sigmoid(g) * _dot(x, wu_ref[0].astype(BF16))).astype(BF16)
        part = _dot(a, wd_ref[0].astype(BF16))

        @pl.when(j == 0)
        def _():
            acc_scr[...] = part

        @pl.when(j > 0)
        def _():
            acc_scr[...] += part

    @pl.when(jnp.logical_and(used, j == last))
    def _():
        if routed:
            o_ref[...] = acc_scr[...]
        else:
            o_ref[...] = acc_scr[...] + res_ref[...]

    @pl.when(jnp.logical_and(jnp.logical_not(used), j == last))
    def _():
        o_ref[...] = jnp.zeros(o_ref.shape, o_ref.dtype)


def _grouped_ffn(tile_expert, n_used, xs, wg, wu, wd, res, tm, tf):
    routed = res is None
    n = xs.shape[0]
    d = wg.shape[1]
    nff = wg.shape[2] // tf

    def jeff(i, j, nu):
        return jnp.where(i < nu[0], j, nff - 1)

    row = lambda i, j, te, nu: (i, 0)
    in_specs = [
        pl.BlockSpec((tm, xs.shape[1]), row),
        pl.BlockSpec((1, d, tf), lambda i, j, te, nu: (te[i], 0, jeff(i, j, nu))),
        pl.BlockSpec((1, d, tf), lambda i, j, te, nu: (te[i], 0, jeff(i, j, nu))),
        pl.BlockSpec((1, tf, d), lambda i, j, te, nu: (te[i], jeff(i, j, nu), 0)),
    ]
    args = [xs, wg, wu, wd]
    scratch = [pltpu.VMEM((tm, d), F32)]
    if routed:
        scratch.append(pltpu.VMEM((tm, d), BF16))
    else:
        in_specs.append(pl.BlockSpec((tm, d), row))
        args.append(res)
    grid_spec = pltpu.PrefetchScalarGridSpec(
        num_scalar_prefetch=2, grid=(n // tm, nff), in_specs=in_specs,
        out_specs=pl.BlockSpec((tm, d), row), scratch_shapes=scratch)
    return pl.pallas_call(
        functools.partial(_ffn_kernel, routed=routed), grid_spec=grid_spec,
        out_shape=jax.ShapeDtypeStruct((n, d), F32),
        compiler_params=_params("parallel", "arbitrary"),
        name="ffn_experts" if routed else "ffn_dense",
    )(tile_expert, n_used, *args)


def _combine_kernel(x_ref, y0_ref, y1_ref, route_ref, o_ref):
    r = route_ref[...]
    w0 = r[:, 2:3]
    w1 = r[:, 3:4]
    o_ref[...] = x_ref[...] + w0 * y0_ref[...] + w1 * y1_ref[...]


def _combine(x2d, y0, y1, route, tm):
    t, d = x2d.shape
    row = lambda i: (i, 0)
    return pl.pallas_call(
        _combine_kernel, grid=(t // tm,),
        in_specs=[pl.BlockSpec((tm, d), row), pl.BlockSpec((tm, d), row),
                  pl.BlockSpec((tm, d), row), pl.BlockSpec((tm, ROUTER_PAD), row)],
        out_specs=pl.BlockSpec((tm, d), row), out_shape=jax.ShapeDtypeStruct((t, d), F32),
        compiler_params=_params("parallel"), name="combine",
    )(x2d, y0, y1, route)


def _rope_tables(seq):
    half = QK_ROPE_DIM // 2
    pos = jnp.arange(seq, dtype=F32)
    inv = jnp.power(ROPE_THETA, -jnp.arange(0, QK_ROPE_DIM, 2, dtype=F32) / QK_ROPE_DIM)
    ang = pos[:, None] * inv[None, :]
    cos, sin = jnp.cos(ang), jnp.sin(ang)
    tail = jnp.zeros((seq, HEAD_PAD - QK_HEAD_DIM), F32)
    c = jnp.concatenate([jnp.ones((seq, QK_NOPE_DIM), F32), cos, cos, tail], axis=1)
    s = jnp.concatenate([jnp.zeros((seq, QK_NOPE_DIM), F32), -sin, sin, tail], axis=1)
    return c, s


def _rope_partner(a):
    half = QK_ROPE_DIM // 2
    lead = a.shape[:-1]
    a = a.reshape(lead + (-1, HEAD_PAD))
    x1 = a[..., QK_NOPE_DIM:QK_NOPE_DIM + half]
    x2 = a[..., QK_NOPE_DIM + half:QK_HEAD_DIM]
    out = jnp.concatenate([jnp.zeros_like(a[..., :QK_NOPE_DIM]), x2, x1,
                           jnp.zeros_like(a[..., QK_HEAD_DIM:])], axis=-1)
    return out.reshape(lead + (-1,))


def _pad_heads(w, per_head, place_at=0):
    k = w.shape[0]
    w = w.reshape(k, N_HEADS, per_head)
    w = jnp.pad(w, ((0, 0), (0, 0), (place_at, HEAD_PAD - per_head - place_at)))
    return w.reshape(k, N_HEADS * HEAD_PAD)


def _layer_params(l, p, rope):
    d = p["w_in"].shape[1]
    w_in, b_in = p["w_in"][l], p["b_in"][l]
    o = [0, Q_LORA_RANK, Q_LORA_RANK + KV_LORA_RANK, Q_LORA_RANK + KV_LORA_RANK + QK_ROPE_DIM]
    o.append(o[3] + S5_WIDTH)
    o.append(o[4] + d)
    pad_rope = lambda a: jnp.pad(a, ((0, 0), (QK_NOPE_DIM, HEAD_PAD - QK_HEAD_DIM)))
    w_kr, b_kr = pad_rope(w_in[:, o[2]:o[3]]), pad_rope(b_in[None, o[2]:o[3]])
    w1 = jnp.concatenate([w_in[:, o[0]:o[2]], w_kr, _rope_partner(w_kr), w_in[:, o[3]:o[4]]], axis=1)
    b1 = jnp.concatenate([b_in[None, o[0]:o[2]], b_kr, _rope_partner(b_kr), b_in[None, o[3]:o[4]]], axis=1)
    w_ukv = p["w_ukv"][l].reshape(KV_LORA_RANK, N_HEADS, QK_NOPE_DIM + V_HEAD_DIM)
    wuq = _pad_heads(p["w_uq"][l], QK_HEAD_DIM)
    pad_g = lambda g: jnp.pad(g[None, :], ((0, 0), (0, HEAD_PAD - QK_HEAD_DIM)))
    qg, kg = pad_g(p["q_head_g"][l]), pad_g(p["k_head_g"][l])
    cos, sin = rope
    qscale = QK_HEAD_DIM ** -0.5 * LOG2E
    head_ones = jnp.kron(jnp.eye(MXU_DIM // HEAD_PAD, dtype=F32), jnp.ones((HEAD_PAD, HEAD_PAD), F32))
    return {
        "ln_mix_g": p["ln_mix_g"][l][None], "ones_d": jnp.ones((d, LANES), BF16),
        "w1": w1.astype(BF16), "b1": b1, "w2": w_in[:, o[4]:].astype(BF16), "b2": b_in[None, o[4]:],
        "q_norm_g": p["q_norm_g"][l][None], "ones_q": jnp.ones((Q_LORA_RANK, LANES), BF16),
        "kv_norm_g": p["kv_norm_g"][l][None], "ones_kv": jnp.ones((KV_LORA_RANK, LANES), BF16),
        "wuq": wuq.astype(BF16), "wuq_rot": _rope_partner(wuq).astype(BF16),
        "wk": _pad_heads(w_ukv[:, :, :QK_NOPE_DIM].reshape(KV_LORA_RANK, -1), QK_NOPE_DIM).astype(BF16),
        "wv": _pad_heads(w_ukv[:, :, QK_NOPE_DIM:].reshape(KV_LORA_RANK, -1), V_HEAD_DIM).astype(BF16),
        "vones": _pad_heads(jnp.ones((1, N_HEADS * (HEAD_PAD - V_HEAD_DIM)), F32),
                            HEAD_PAD - V_HEAD_DIM, place_at=V_HEAD_DIM),
        "head_ones": head_ones.astype(BF16),
        "qa": qg * cos * qscale, "qb": _rope_partner(qg) * sin * qscale,
        "ka": kg * cos, "kb": _rope_partner(kg) * sin,
        "s5_d": p["s5_d"][l][None], "w_glu": p["w_glu"][l].astype(BF16), "b_glu": p["b_glu"][l][None],
        "w_proj_attn": p["w_proj_attn"][l].astype(BF16), "w_proj_ssm": p["w_proj_ssm"][l].astype(BF16),
        "w_out": p["w_out"][l].astype(BF16), "ln_ffn_g": p["ln_ffn_g"][l][None],
    }


def _routing(route, tm):
    t = route.shape[0]
    e = route[:, :TOP_K].astype(jnp.int32)
    seli = jnp.sum((e[:, :, None] == jnp.arange(N_EXPERTS)[None, None, :]).astype(jnp.int32), axis=1)
    cnt = jnp.sum(seli, axis=0)
    padded = ((cnt + tm - 1) // tm) * tm
    ends = jnp.cumsum(padded)
    pos_te = (ends - padded)[None, :] + jnp.cumsum(seli, axis=0) - seli
    slots = jnp.take_along_axis(pos_te, e, axis=1)
    n_slots = TOP_K * t + N_EXPERTS * tm
    tok = jnp.broadcast_to(jnp.arange(t, dtype=jnp.int32)[:, None], (t, TOP_K))
    tok_of_slot = jnp.zeros((n_slots,), jnp.int32).at[slots.reshape(-1)].set(
        tok.reshape(-1), unique_indices=True)
    tile_start = jnp.arange(n_slots // tm, dtype=jnp.int32) * tm
    tile_expert = jnp.sum((tile_start[:, None] >= ends[None, :]).astype(jnp.int32), axis=1)
    tile_expert = jnp.minimum(tile_expert, N_EXPERTS - 1)
    n_used = (ends[-1] // tm).astype(jnp.int32)[None]
    tile_expert = jnp.where(tile_start < ends[-1], tile_expert, tile_expert[jnp.maximum(n_used[0] - 1, 0)])
    return tok_of_slot, tile_expert.astype(jnp.int32), n_used, slots


def kernel(x, ln_mix_g, w_in, b_in, q_norm_g, w_uq, kv_norm_g, w_ukv, q_head_g, k_head_g,
           s5_lam_re, s5_lam_im, s5_log_step, s5_b_re, s5_b_im, s5_c_re, s5_c_im, s5_d,
           w_glu, b_glu, w_proj_attn, w_proj_ssm, w_out, ln_ffn_g,
           ffn_w_gate, ffn_w_up, ffn_w_down, router_w, router_b,
           moe_w_gate, moe_w_up, moe_w_down):
    batch, seq, d = x.shape
    depth = w_in.shape[0]
    t = batch * seq
    tm = min(512, seq)
    tq = min(512, seq)
    tm_ffn = min(1024, t)
    chunks_per_seq = seq // S5_CHUNK
    assert chunks_per_seq & (chunks_per_seq - 1) == 0 and seq % tm == 0 and t % tm_ffn == 0
    p = dict(ln_mix_g=ln_mix_g, w_in=w_in, b_in=b_in, q_norm_g=q_norm_g, w_uq=w_uq,
             kv_norm_g=kv_norm_g, w_ukv=w_ukv, q_head_g=q_head_g, k_head_g=k_head_g, s5_d=s5_d,
             w_glu=w_glu, b_glu=b_glu, w_proj_attn=w_proj_attn, w_proj_ssm=w_proj_ssm,
             w_out=w_out, ln_ffn_g=ln_ffn_g)
    rope = _rope_tables(seq)
    x2d = x.reshape(t, d)
    for l in range(depth):
        lp = _layer_params(l, p, rope)
        q, k, v, u, ga, gb = _inproj(x2d, lp, seq, tm)
        attn = _attention(q, k, v, batch, seq, tq)
        mats = _s5_matrices(s5_lam_re[l], s5_lam_im[l], s5_log_step[l], s5_b_re[l], s5_b_im[l],
                            s5_c_re[l], s5_c_im[l], chunks_per_seq)
        ug = (u.astype(BF16).reshape(t // S5_CHUNK, S5_CHUNK, S5_GROUPS, S5_GROUP)
              .transpose(2, 0, 1, 3).reshape(S5_GROUPS, t // S5_CHUNK, S5_CHUNK * S5_GROUP))
        yg = _s5_scan(ug, mats, chunks_per_seq)
        ys = (yg.reshape(S5_GROUPS, t // S5_CHUNK, S5_CHUNK, S5_GROUP)
              .transpose(1, 2, 0, 3).reshape(t, S5_WIDTH))
        j = l // 2
        if l % 2 == 0:
            x2d, h2 = _mix(x2d, attn, ys, u, ga, gb, lp, tm)
            n_tiles = t // tm_ffn
            x2d = _grouped_ffn(
                jnp.zeros((n_tiles,), jnp.int32), jnp.full((1,), n_tiles, jnp.int32), h2,
                ffn_w_gate[j][None], ffn_w_up[j][None], ffn_w_down[j][None], x2d, tm_ffn, 256)
        else:
            rw = jnp.pad(router_w[j], ((0, 0), (0, ROUTER_PAD - N_EXPERTS)))
            rb = jnp.pad(router_b[j][None], ((0, 0), (0, ROUTER_PAD - N_EXPERTS)), constant_values=NEG_BIG)
            x2d, h2, route = _mix(x2d, attn, ys, u, ga, gb, lp, tm, router=(rw, rb))
            tok_of_slot, tile_expert, n_used, slots = _routing(route, tm_ffn)
            xs = jnp.take(h2, tok_of_slot, axis=0)
            ysort = _grouped_ffn(tile_expert, n_used, xs, moe_w_gate[j], moe_w_up[j], moe_w_down[j],
                                 None, tm_ffn, 512)
            x2d = _combine(x2d, jnp.take(ysort, slots[:, 0], axis=0),
                           jnp.take(ysort, slots[:, 1], axis=0), route, tm)
    return x2d.reshape(batch, seq, d)
```

```python
import functools
import math

import jax
import jax.numpy as jnp
from jax import lax
from jax.experimental import pallas as pl
from jax.experimental.pallas import tpu as pltpu

EPS = 1e-6
N_HEADS = 8
QK_NOPE_DIM = 64
QK_ROPE_DIM = 32
QK_HEAD_DIM = QK_NOPE_DIM + QK_ROPE_DIM
V_HEAD_DIM = 64
LANES = 128
HEAD_PAD = LANES
MXU_DIM = 256
Q_LORA_RANK = 256
KV_LORA_RANK = 128
ROPE_THETA = 10000.0
S5_WIDTH = 512
S5_GROUP = 16
S5_GROUPS = S5_WIDTH // S5_GROUP
S5_STATE = 64
S5_CHUNK = 16
N_EXPERTS = 8
TOP_K = 2
ROUTER_PAD = LANES
EXPERT_CALLS = 4
NEG_BIG = -1e30
LOG2E = 1.4426950408889634

VMEM_LIMIT_BYTES = 56 * 1024 * 1024

F32 = jnp.float32
BF16 = jnp.bfloat16


def _dot(a, b):
    return jnp.dot(a, b, preferred_element_type=F32)


def _lane_tile(a, n):
    return jnp.concatenate([a] * n, axis=1)


def _rms_scale(x, ones_ref):
    ssq = _dot((x * x).astype(BF16), ones_ref[...])
    return lax.rsqrt(ssq * (1.0 / x.shape[1]) + EPS)


def _rms(x, g, ones_ref):
    return x * _lane_tile(_rms_scale(x, ones_ref), x.shape[1] // LANES) * g


def _params(*semantics):
    return pltpu.CompilerParams(dimension_semantics=semantics, vmem_limit_bytes=VMEM_LIMIT_BYTES)


def _full(shape):
    return pl.BlockSpec(shape, lambda *_: (0,) * len(shape))


def _inproj_kernel(x_ref, g_ref, onesd_ref, w1_ref, b1_ref, w2_ref, b2_ref, qng_ref, onesq_ref,
                   kvng_ref, oneskv_ref, wuq_ref, wuqr_ref, wk_ref, wv_ref, vones_ref, hones_ref,
                   qa_ref, qb_ref, ka_ref, kb_ref,
                   q_ref, k_ref, v_ref, u_ref, ga_ref, gb_ref):
    d = x_ref.shape[1]
    h = _rms(x_ref[...], g_ref[...], onesd_ref).astype(BF16)
    p1 = _dot(h, w1_ref[...]) + b1_ref[...]
    p2 = _dot(h, w2_ref[...]) + b2_ref[...]
    ga_ref[...] = jax.nn.sigmoid(p2[:, :d]).astype(BF16)
    gb_ref[...] = jax.nn.sigmoid(p2[:, d:]).astype(BF16)
    c0 = Q_LORA_RANK
    c1 = c0 + KV_LORA_RANK
    c2 = c1 + HEAD_PAD
    c3 = c2 + HEAD_PAD
    u_ref[...] = p1[:, c3:]
    qn = _rms(p1[:, :c0], qng_ref[...], onesq_ref).astype(BF16)
    kvn = _rms(p1[:, c0:c1], kvng_ref[...], oneskv_ref).astype(BF16)
    v_ref[...] = (_dot(kvn, wv_ref[...]) + vones_ref[...]).astype(BF16)

    def head_norm_rope(raw, rot, a_tab, b_tab):
        chunks = []
        for c in range(raw.shape[1] // MXU_DIM):
            blk = raw[:, c * MXU_DIM:(c + 1) * MXU_DIM]
            chunks.append(_dot((blk * blk).astype(BF16), hones_ref[...]))
        rs = lax.rsqrt(jnp.concatenate(chunks, axis=1) * (1.0 / QK_HEAD_DIM) + EPS)
        return rs * (raw * _lane_tile(a_tab, N_HEADS) + rot * _lane_tile(b_tab, N_HEADS))

    q_raw = _dot(qn, wuq_ref[...])
    q_rot = _dot(qn, wuqr_ref[...])
    q_ref[...] = head_norm_rope(q_raw, q_rot, qa_ref[...], qb_ref[...]).astype(BF16)
    k_raw = _dot(kvn, wk_ref[...]) + _lane_tile(p1[:, c1:c2], N_HEADS)
    k_rot = _lane_tile(p1[:, c2:c3], N_HEADS)
    k_ref[...] = head_norm_rope(k_raw, k_rot, ka_ref[...], kb_ref[...]).astype(BF16)


def _inproj(x2d, lp, seq, tm):
    t, d = x2d.shape
    nseq = seq // tm
    hp = N_HEADS * HEAD_PAD
    row = lambda i: (i, 0)
    pos = lambda i: (i % nseq, 0)
    consts = [lp[k] for k in ("ln_mix_g", "ones_d", "w1", "b1", "w2", "b2", "q_norm_g", "ones_q",
                              "kv_norm_g", "ones_kv", "wuq", "wuq_rot", "wk", "wv", "vones", "head_ones")]
    tabs = [lp[k] for k in ("qa", "qb", "ka", "kb")]
    in_specs = ([pl.BlockSpec((tm, d), row)] + [_full(c.shape) for c in consts]
                + [pl.BlockSpec((tm, HEAD_PAD), pos)] * 4)
    wide = lambda n, dt: (jax.ShapeDtypeStruct((t, n), dt), pl.BlockSpec((tm, n), row))
    outs = [wide(hp, BF16), wide(hp, BF16), wide(hp, BF16), wide(S5_WIDTH, F32), wide(d, BF16), wide(d, BF16)]
    return pl.pallas_call(
        _inproj_kernel, grid=(t // tm,), in_specs=in_specs,
        out_specs=tuple(o[1] for o in outs), out_shape=tuple(o[0] for o in outs),
        compiler_params=_params("parallel"), name="inproj",
    )(x2d, *consts, *tabs)


def _attn_kernel(q_ref, k_ref, v_ref, o_ref, m_scr, acc_scr):
    i = pl.program_id(1)
    j = pl.program_id(2)
    tq = q_ref.shape[0]
    tk = k_ref.shape[0]

    @pl.when(j == 0)
    def _():
        m_scr[...] = jnp.full(m_scr.shape, -jnp.inf, F32)
        acc_scr[...] = jnp.zeros(acc_scr.shape, F32)

    def step(masked):
        if masked:
            rows = lax.broadcasted_iota(jnp.int32, (tq, tk), 0)
            cols = lax.broadcasted_iota(jnp.int32, (tq, tk), 1)
            keep = cols <= rows
        for hd in range(N_HEADS):
            sl = slice(hd * HEAD_PAD, (hd + 1) * HEAD_PAD)
            s = lax.dot_general(q_ref[:, sl], k_ref[:, sl], (((1,), (1,)), ((), ())),
                                preferred_element_type=F32)
            if masked:
                s = jnp.where(keep, s, -jnp.inf)
            parts = [s[:, c * LANES:(c + 1) * LANES] for c in range(tk // LANES)]
            blk_max = functools.reduce(jnp.maximum, parts)
            m_prev = m_scr[hd]
            m_new = jnp.maximum(m_prev, jnp.max(blk_max, axis=-1, keepdims=True))
            alpha = jnp.exp2(m_prev - m_new)
            p = jnp.concatenate([jnp.exp2(part - m_new) for part in parts], axis=1).astype(BF16)
            acc_scr[hd] = alpha * acc_scr[hd] + _dot(p, v_ref[:, sl])
            m_scr[hd] = m_new

    @pl.when(j < i)
    def _():
        step(False)

    @pl.when(j == i)
    def _():
        step(True)
        lane = lax.broadcasted_iota(jnp.int32, (tq, HEAD_PAD), 1)
        for pr in range(N_HEADS // 2):
            a0 = acc_scr[2 * pr]
            a1 = acc_scr[2 * pr + 1]
            lo = a0 / pltpu.roll(a0, V_HEAD_DIM, 1)
            hi = pltpu.roll(a1, V_HEAD_DIM, 1) / a1
            o_ref[:, pr * HEAD_PAD:(pr + 1) * HEAD_PAD] = jnp.where(lane < V_HEAD_DIM, lo, hi).astype(BF16)


def _attention(q, k, v, batch, seq, tq):
    nq = seq // tq
    hp = N_HEADS * HEAD_PAD
    vw = N_HEADS * V_HEAD_DIM
    qmap = lambda b, i, j: (b * nq + i, 0)
    kmap = lambda b, i, j: (b * nq + jnp.minimum(j, i), 0)
    return pl.pallas_call(
        _attn_kernel, grid=(batch, nq, nq),
        in_specs=[pl.BlockSpec((tq, hp), qmap), pl.BlockSpec((tq, hp), kmap),
                  pl.BlockSpec((tq, hp), kmap)],
        out_specs=pl.BlockSpec((tq, vw), qmap),
        out_shape=jax.ShapeDtypeStruct((batch * seq, vw), BF16),
        scratch_shapes=[pltpu.VMEM((N_HEADS, tq, HEAD_PAD), F32), pltpu.VMEM((N_HEADS, tq, HEAD_PAD), F32)],
        compiler_params=_params("parallel", "parallel", "arbitrary"), name="attention",
    )(q, k, v)


def _s5_kernel(u_ref, tcat_ref, p_ref, q_ref, a_ref, y_ref, acc_scr):
    ln = S5_CHUNK
    chunks = acc_scr.shape[0]
    half = a_ref.shape[2] // 2
    z = jnp.zeros((chunks, 2 * half), F32)
    for s in range(ln):
        us = u_ref[pl.ds(s, chunks, stride=ln), :].astype(BF16)
        contrib = _dot(us, tcat_ref[0, :, :(ln - s) * LANES])
        if s == 0:
            acc_scr[...] = contrib
        else:
            acc_scr[:, s * LANES:] += contrib
        z = z + _dot(us, p_ref[0, s])
    row = lax.broadcasted_iota(jnp.int32, z.shape, 0)
    a = a_ref[0]
    off = 1
    lvl = 0
    while off < chunks:
        prev = pltpu.roll(z, off, 0)
        contrib = prev * a[2 * lvl:2 * lvl + 1] + pltpu.roll(prev, half, 1) * a[2 * lvl + 1:2 * lvl + 2]
        z = z + jnp.where(row >= off, contrib, 0.0)
        off *= 2
        lvl += 1
    x_start = jnp.where(row >= 1, pltpu.roll(z, 1, 0), 0.0)
    y = acc_scr[...] + _dot(x_start.astype(BF16), q_ref[0])
    for t in range(ln):
        y_ref[pl.ds(t, chunks, stride=ln), :] = y[:, t * LANES:(t + 1) * LANES]


def _s5_scan(u, mats, batch, seq):
    t, width = u.shape
    tcat, p, q, a = mats
    nblk = width // LANES
    blk = lambda arr: pl.BlockSpec((1,) + arr.shape[1:], lambda g, b: (g,) + (0,) * (arr.ndim - 1))
    act = pl.BlockSpec((seq, LANES), lambda g, b: (b, g))
    return pl.pallas_call(
        _s5_kernel, grid=(nblk, batch),
        in_specs=[act, blk(tcat), blk(p), blk(q), blk(a)],
        out_specs=act, out_shape=jax.ShapeDtypeStruct((t, width), F32),
        scratch_shapes=[pltpu.VMEM((seq // S5_CHUNK, S5_CHUNK * LANES), F32)],
        compiler_params=_params("parallel", "parallel"), name="s5_scan",
    )(u, tcat, p, q, a)


def _s5_matrices(lam_re, lam_im, log_step, b_re, b_im, c_re, c_im, chunks_per_seq):
    hi = lax.Precision.HIGHEST
    ln = S5_CHUNK
    gpb = LANES // S5_GROUP
    nblk = S5_GROUPS // gpb
    step = jnp.exp(log_step)[:, None]
    ar = lam_re * step
    ai = lam_im * step
    lbr = jnp.exp(ar) * jnp.cos(ai)
    lbi = jnp.exp(ar) * jnp.sin(ai)
    den = lam_re * lam_re + lam_im * lam_im
    cr = ((lbr - 1.0) * lam_re + lbi * lam_im) / den
    ci = (lbi * lam_re - (lbr - 1.0) * lam_im) / den
    bbr = cr[..., None] * b_re - ci[..., None] * b_im
    bbi = cr[..., None] * b_im + ci[..., None] * b_re
    dd = jnp.arange(ln + 1, dtype=F32)[:, None, None]
    pr = jnp.exp(dd * ar) * jnp.cos(dd * ai)
    pi = jnp.exp(dd * ar) * jnp.sin(dd * ai)
    tr = pr[:ln, :, :, None] * bbr - pi[:ln, :, :, None] * bbi
    ti = pr[:ln, :, :, None] * bbi + pi[:ln, :, :, None] * bbr
    kd = (jnp.einsum("gcp,dgpe->dgce", c_re, tr, precision=hi)
          - jnp.einsum("gcp,dgpe->dgce", c_im, ti, precision=hi))
    eye = jnp.eye(gpb, dtype=F32)

    def block_diag(x):
        lead = x.shape[:-3]
        r, c = x.shape[-2:]
        x = x.reshape(lead + (nblk, gpb, r, c))
        x = x[..., :, :, None, :] * eye[:, None, :, None]
        return x.reshape(lead + (nblk, gpb * r, gpb * c))

    tcat = block_diag(kd.transpose(0, 1, 3, 2)).transpose(1, 2, 0, 3).reshape(nblk, LANES, ln * LANES)
    p = jnp.concatenate([block_diag(tr[::-1].transpose(0, 1, 3, 2)),
                         block_diag(ti[::-1].transpose(0, 1, 3, 2))], axis=-1).transpose(1, 0, 2, 3)
    q_re = c_re[None] * pr[1:, :, None, :] - c_im[None] * pi[1:, :, None, :]
    q_im = -(c_re[None] * pi[1:, :, None, :] + c_im[None] * pr[1:, :, None, :])
    to_q = lambda x: block_diag(x.transpose(0, 1, 3, 2)).transpose(1, 2, 0, 3).reshape(
        nblk, gpb * S5_STATE, ln * LANES)
    q = jnp.concatenate([to_q(q_re), to_q(q_im)], axis=1)
    levels = int(math.log2(chunks_per_seq))
    e = (ln * (2.0 ** jnp.arange(levels, dtype=F32)))[:, None, None]
    er = (jnp.exp(e * ar) * jnp.cos(e * ai)).reshape(levels, nblk, gpb * S5_STATE)
    ei = (jnp.exp(e * ar) * jnp.sin(e * ai)).reshape(levels, nblk, gpb * S5_STATE)
    a = jnp.stack([jnp.concatenate([er, er], -1), jnp.concatenate([-ei, ei], -1)], axis=1)
    a = a.transpose(2, 0, 1, 3).reshape(nblk, 2 * levels, 2 * gpb * S5_STATE)
    return tcat.astype(BF16), p.astype(BF16), q.astype(BF16), a


def _split_bf16(a):
    hi = a.astype(BF16)
    return hi, (a - hi.astype(F32)).astype(BF16)


def _mix_kernel(*refs, routed):
    (x_ref, attn_ref, ys_ref, u_ref, ga_ref, gb_ref, d_ref, wglu_ref, bglu_ref, wpa_ref, wps_ref,
     wout_ref, lng_ref, onesd_ref) = refs[:14]
    if routed:
        rw_ref, rb_ref, xo_ref, h_ref, route_ref = refs[14:]
    else:
        xo_ref, h_ref = refs[14:]
    y = ys_ref[...] + d_ref[...] * u_ref[...]
    y = jax.nn.gelu(y, approximate=True)
    ssm = y * jax.nn.sigmoid(_dot(y.astype(BF16), wglu_ref[...]) + bglu_ref[...])
    pa = _dot(attn_ref[...], wpa_ref[...])
    ps = _dot(ssm.astype(BF16), wps_ref[...])
    mix = ga_ref[...].astype(F32) * pa + gb_ref[...].astype(F32) * ps
    xn = x_ref[...] + _dot(mix.astype(BF16), wout_ref[...])
    xo_ref[...] = xn
    h2 = _rms(xn, lng_ref[...], onesd_ref)
    if not routed:
        h_ref[...] = h2.astype(BF16)
        return
    h_ref[...] = h2
    h_hi, h_lo = _split_bf16(h2)
    r_hi, r_lo = _split_bf16(rw_ref[...])
    logits = _dot(h_hi, r_hi) + _dot(h_lo, r_hi) + _dot(h_hi, r_lo) + rb_ref[...]
    lane = lax.broadcasted_iota(jnp.int32, logits.shape, 1).astype(F32)
    m1 = jnp.max(logits, axis=-1, keepdims=True)
    i1 = jnp.min(jnp.where(logits == m1, lane, float(ROUTER_PAD)), axis=-1, keepdims=True)
    rest = jnp.where(lane == i1, NEG_BIG, logits)
    m2 = jnp.max(rest, axis=-1, keepdims=True)
    i2 = jnp.min(jnp.where(rest == m2, lane, float(ROUTER_PAD)), axis=-1, keepdims=True)
    e2 = jnp.exp(m2 - m1)
    w1 = 1.0 / (1.0 + e2)
    route_ref[...] = jnp.where(lane == 0.0, i1, jnp.where(lane == 1.0, i2, jnp.where(
        lane == 2.0, w1, jnp.where(lane == 3.0, e2 * w1, 0.0))))


def _mix(x2d, attn, ys, u, ga, gb, lp, tm, router=None):
    t, d = x2d.shape
    row = lambda i: (i, 0)
    routed = router is not None
    consts = [lp[k] for k in ("s5_d", "w_glu", "b_glu", "w_proj_attn", "w_proj_ssm", "w_out",
                              "ln_ffn_g", "ones_d")]
    if routed:
        consts += list(router)
    acts = [x2d, attn, ys, u, ga, gb]
    in_specs = [pl.BlockSpec((tm, a.shape[1]), row) for a in acts] + [_full(c.shape) for c in consts]
    if routed:
        out_shape = [jax.ShapeDtypeStruct((t, d), F32), jax.ShapeDtypeStruct((t, d), F32),
                     jax.ShapeDtypeStruct((t, ROUTER_PAD), F32)]
    else:
        out_shape = [jax.ShapeDtypeStruct((t, d), F32), jax.ShapeDtypeStruct((t, d), BF16)]
    out_specs = [pl.BlockSpec((tm, o.shape[1]), row) for o in out_shape]
    return pl.pallas_call(
        functools.partial(_mix_kernel, routed=routed), grid=(t // tm,), in_specs=in_specs,
        out_specs=tuple(out_specs), out_shape=tuple(out_shape),
        compiler_params=_params("parallel"), name="mix_routed" if routed else "mix",
    )(*acts, *consts)


def _ffn_kernel(te_ref, nu_ref, *refs, routed):
    if routed:
        x_ref, wg_ref, wu_ref, wd_ref, _, o_ref, acc_scr, x_scr = refs
    else:
        x_ref, wg_ref, wu_ref, wd_ref, res_ref, o_ref, acc_scr = refs
    i = pl.program_id(0)
    j = pl.program_id(1)
    last = pl.num_programs(1) - 1
    used = i < nu_ref[0]

    if routed:
        @pl.when(jnp.logical_and(used, j == 0))
        def _():
            x_scr[...] = x_ref[...].astype(BF16)

    @pl.when(used)
    def _():
        x = x_scr[...] if routed else x_ref[...]
        g = _dot(x, wg_ref[0].astype(BF16))
        a = (g * jax.nn.sigmoid(g) * _dot(x, wu_ref[0].astype(BF16))).astype(BF16)
        part = _dot(a, wd_ref[0].astype(BF16))

        @pl.when(j == 0)
        def _():
            acc_scr[...] = part

        @pl.when(j > 0)
        def _():
            acc_scr[...] += part

    @pl.when(jnp.logical_and(used, j == last))
    def _():
        if routed:
            o_ref[...] = acc_scr[...]
        else:
            o_ref[...] = acc_scr[...] + res_ref[...]

    @pl.when(jnp.logical_and(jnp.logical_not(used), j == last))
    def _():
        o_ref[...] = jnp.zeros(o_ref.shape, o_ref.dtype)


def _grouped_ffn(tile_expert, n_used, xs, wg, wu, wd, res, tm, tf, ybuf=None, tile_offset=0):
    routed = res is None
    n = xs.shape[0]
    d = wg.shape[1]
    nff = wg.shape[2] // tf

    def jeff(i, j, nu):
        return jnp.where(i < nu[0], j, nff - 1)

    row = lambda i, j, te, nu: (i, 0)
    in_specs = [
        pl.BlockSpec((tm, xs.shape[1]), row),
        pl.BlockSpec((1, d, tf), lambda i, j, te, nu: (te[i], 0, jeff(i, j, nu))),
        pl.BlockSpec((1, d, tf), lambda i, j, te, nu: (te[i], 0, jeff(i, j, nu))),
        pl.BlockSpec((1, tf, d), lambda i, j, te, nu: (te[i], jeff(i, j, nu), 0)),
    ]
    args = [xs, wg, wu, wd]
    scratch = [pltpu.VMEM((tm, d), F32)]
    aliases = {}
    if routed:
        scratch.append(pltpu.VMEM((tm, d), BF16))
        in_specs.append(pl.BlockSpec(memory_space=pl.ANY))
        args.append(ybuf)
        aliases = {2 + len(args) - 1: 0}
        out_rows = ybuf.shape[0]
    else:
        in_specs.append(pl.BlockSpec((tm, d), row))
        args.append(res)
        out_rows = n
    grid_spec = pltpu.PrefetchScalarGridSpec(
        num_scalar_prefetch=2, grid=(n // tm, nff), in_specs=in_specs,
        out_specs=pl.BlockSpec((tm, d), lambda i, j, te, nu: (i + tile_offset, 0)), scratch_shapes=scratch)
    return pl.pallas_call(
        functools.partial(_ffn_kernel, routed=routed), grid_spec=grid_spec,
        out_shape=jax.ShapeDtypeStruct((out_rows, d), F32), input_output_aliases=aliases,
        compiler_params=_params("parallel", "arbitrary"),
        name="ffn_experts" if routed else "ffn_dense",
    )(tile_expert, n_used, *args)


def _combine_kernel(x_ref, y0_ref, y1_ref, route_ref, o_ref):
    r = route_ref[...]
    w0 = r[:, 2:3]
    w1 = r[:, 3:4]
    o_ref[...] = x_ref[...] + w0 * y0_ref[...] + w1 * y1_ref[...]


def _combine(x2d, y0, y1, route, tm):
    t, d = x2d.shape
    row = lambda i: (i, 0)
    return pl.pallas_call(
        _combine_kernel, grid=(t // tm,),
        in_specs=[pl.BlockSpec((tm, d), row), pl.BlockSpec((tm, d), row),
                  pl.BlockSpec((tm, d), row), pl.BlockSpec((tm, ROUTER_PAD), row)],
        out_specs=pl.BlockSpec((tm, d), row), out_shape=jax.ShapeDtypeStruct((t, d), F32),
        compiler_params=_params("parallel"), name="combine",
    )(x2d, y0, y1, route)


def _rope_tables(seq):
    half = QK_ROPE_DIM // 2
    pos = jnp.arange(seq, dtype=F32)
    inv = jnp.power(ROPE_THETA, -jnp.arange(0, QK_ROPE_DIM, 2, dtype=F32) / QK_ROPE_DIM)
    ang = pos[:, None] * inv[None, :]
    cos, sin = jnp.cos(ang), jnp.sin(ang)
    tail = jnp.zeros((seq, HEAD_PAD - QK_HEAD_DIM), F32)
    c = jnp.concatenate([jnp.ones((seq, QK_NOPE_DIM), F32), cos, cos, tail], axis=1)
    s = jnp.concatenate([jnp.zeros((seq, QK_NOPE_DIM), F32), -sin, sin, tail], axis=1)
    return c, s


def _rope_partner(a):
    half = QK_ROPE_DIM // 2
    lead = a.shape[:-1]
    a = a.reshape(lead + (-1, HEAD_PAD))
    x1 = a[..., QK_NOPE_DIM:QK_NOPE_DIM + half]
    x2 = a[..., QK_NOPE_DIM + half:QK_HEAD_DIM]
    out = jnp.concatenate([jnp.zeros_like(a[..., :QK_NOPE_DIM]), x2, x1,
                           jnp.zeros_like(a[..., QK_HEAD_DIM:])], axis=-1)
    return out.reshape(lead + (-1,))


def _pad_heads(w, per_head, place_at=0):
    k = w.shape[0]
    w = w.reshape(k, N_HEADS, per_head)
    w = jnp.pad(w, ((0, 0), (0, 0), (place_at, HEAD_PAD - per_head - place_at)))
    return w.reshape(k, N_HEADS * HEAD_PAD)


def _layer_params(l, p, rope):
    d = p["w_in"].shape[1]
    w_in, b_in = p["w_in"][l], p["b_in"][l]
    o = [0, Q_LORA_RANK, Q_LORA_RANK + KV_LORA_RANK, Q_LORA_RANK + KV_LORA_RANK + QK_ROPE_DIM]
    o.append(o[3] + S5_WIDTH)
    o.append(o[4] + d)
    pad_rope = lambda a: jnp.pad(a, ((0, 0), (QK_NOPE_DIM, HEAD_PAD - QK_HEAD_DIM)))
    w_kr, b_kr = pad_rope(w_in[:, o[2]:o[3]]), pad_rope(b_in[None, o[2]:o[3]])
    w1 = jnp.concatenate([w_in[:, o[0]:o[2]], w_kr, _rope_partner(w_kr), w_in[:, o[3]:o[4]]], axis=1)
    b1 = jnp.concatenate([b_in[None, o[0]:o[2]], b_kr, _rope_partner(b_kr), b_in[None, o[3]:o[4]]], axis=1)
    w_ukv = p["w_ukv"][l].reshape(KV_LORA_RANK, N_HEADS, QK_NOPE_DIM + V_HEAD_DIM)
    wuq = _pad_heads(p["w_uq"][l], QK_HEAD_DIM)
    pad_g = lambda g: jnp.pad(g[None, :], ((0, 0), (0, HEAD_PAD - QK_HEAD_DIM)))
    qg, kg = pad_g(p["q_head_g"][l]), pad_g(p["k_head_g"][l])
    cos, sin = rope
    qscale = QK_HEAD_DIM ** -0.5 * LOG2E
    head_ones = jnp.kron(jnp.eye(MXU_DIM // HEAD_PAD, dtype=F32), jnp.ones((HEAD_PAD, HEAD_PAD), F32))
    return {
        "ln_mix_g": p["ln_mix_g"][l][None], "ones_d": jnp.ones((d, LANES), BF16),
        "w1": w1.astype(BF16), "b1": b1, "w2": w_in[:, o[4]:].astype(BF16), "b2": b_in[None, o[4]:],
        "q_norm_g": p["q_norm_g"][l][None], "ones_q": jnp.ones((Q_LORA_RANK, LANES), BF16),
        "kv_norm_g": p["kv_norm_g"][l][None], "ones_kv": jnp.ones((KV_LORA_RANK, LANES), BF16),
        "wuq": wuq.astype(BF16), "wuq_rot": _rope_partner(wuq).astype(BF16),
        "wk": _pad_heads(w_ukv[:, :, :QK_NOPE_DIM].reshape(KV_LORA_RANK, -1), QK_NOPE_DIM).astype(BF16),
        "wv": _pad_heads(w_ukv[:, :, QK_NOPE_DIM:].reshape(KV_LORA_RANK, -1), V_HEAD_DIM).astype(BF16),
        "vones": _pad_heads(jnp.ones((1, N_HEADS * (HEAD_PAD - V_HEAD_DIM)), F32),
                            HEAD_PAD - V_HEAD_DIM, place_at=V_HEAD_DIM),
        "head_ones": head_ones.astype(BF16),
        "qa": qg * cos * qscale, "qb": _rope_partner(qg) * sin * qscale,
        "ka": kg * cos, "kb": _rope_partner(kg) * sin,
        "s5_d": p["s5_d"][l][None], "w_glu": p["w_glu"][l].astype(BF16), "b_glu": p["b_glu"][l][None],
        "w_proj_attn": p["w_proj_attn"][l].astype(BF16), "w_proj_ssm": p["w_proj_ssm"][l].astype(BF16),
        "w_out": p["w_out"][l].astype(BF16), "ln_ffn_g": p["ln_ffn_g"][l][None],
    }


def _routing(route, tm):
    t = route.shape[0]
    e = route[:, :TOP_K].astype(jnp.int32)
    seli = jnp.sum((e[:, :, None] == jnp.arange(N_EXPERTS)[None, None, :]).astype(jnp.int32), axis=1)
    cnt = jnp.sum(seli, axis=0)
    padded = ((cnt + tm - 1) // tm) * tm
    ends = jnp.cumsum(padded)
    pos_te = (ends - padded)[None, :] + jnp.cumsum(seli, axis=0) - seli
    slots = jnp.take_along_axis(pos_te, e, axis=1)
    n_slots = TOP_K * t + N_EXPERTS * tm
    tok = jnp.broadcast_to(jnp.arange(t, dtype=jnp.int32)[:, None], (t, TOP_K))
    tok_of_slot = jnp.zeros((n_slots,), jnp.int32).at[slots.reshape(-1)].set(
        tok.reshape(-1), unique_indices=True)
    tile_start = jnp.arange(n_slots // tm, dtype=jnp.int32) * tm
    tile_expert = jnp.sum((tile_start[:, None] >= ends[None, :]).astype(jnp.int32), axis=1)
    tile_expert = jnp.minimum(tile_expert, N_EXPERTS - 1)
    n_used = (ends[-1] // tm).astype(jnp.int32)[None]
    tile_expert = jnp.where(tile_start < ends[-1], tile_expert, tile_expert[jnp.maximum(n_used[0] - 1, 0)])
    return tok_of_slot, tile_expert.astype(jnp.int32), n_used, slots


def kernel(x, ln_mix_g, w_in, b_in, q_norm_g, w_uq, kv_norm_g, w_ukv, q_head_g, k_head_g,
           s5_lam_re, s5_lam_im, s5_log_step, s5_b_re, s5_b_im, s5_c_re, s5_c_im, s5_d,
           w_glu, b_glu, w_proj_attn, w_proj_ssm, w_out, ln_ffn_g,
           ffn_w_gate, ffn_w_up, ffn_w_down, router_w, router_b,
           moe_w_gate, moe_w_up, moe_w_down):
    batch, seq, d = x.shape
    depth = w_in.shape[0]
    t = batch * seq
    tm = min(512, seq)
    tq = min(512, seq)
    tm_ffn = min(1024, t)
    chunks_per_seq = seq // S5_CHUNK
    assert chunks_per_seq & (chunks_per_seq - 1) == 0 and seq % tm == 0 and t % tm_ffn == 0
    p = dict(ln_mix_g=ln_mix_g, w_in=w_in, b_in=b_in, q_norm_g=q_norm_g, w_uq=w_uq,
             kv_norm_g=kv_norm_g, w_ukv=w_ukv, q_head_g=q_head_g, k_head_g=k_head_g, s5_d=s5_d,
             w_glu=w_glu, b_glu=b_glu, w_proj_attn=w_proj_attn, w_proj_ssm=w_proj_ssm,
             w_out=w_out, ln_ffn_g=ln_ffn_g)
    rope = _rope_tables(seq)
    x2d = x.reshape(t, d)
    for l in range(depth):
        lp = _layer_params(l, p, rope)
        q, k, v, u, ga, gb = _inproj(x2d, lp, seq, tm)
        attn = _attention(q, k, v, batch, seq, tq)
        mats = _s5_matrices(s5_lam_re[l], s5_lam_im[l], s5_log_step[l], s5_b_re[l], s5_b_im[l],
                            s5_c_re[l], s5_c_im[l], chunks_per_seq)
        ys = _s5_scan(u, mats, batch, seq)
        j = l // 2
        if l % 2 == 0:
            x2d, h2 = _mix(x2d, attn, ys, u, ga, gb, lp, tm)
            n_tiles = t // tm_ffn
            x2d = _grouped_ffn(
                jnp.zeros((n_tiles,), jnp.int32), jnp.full((1,), n_tiles, jnp.int32), h2,
                ffn_w_gate[j][None], ffn_w_up[j][None], ffn_w_down[j][None], x2d, tm_ffn, 256)
        else:
            rw = jnp.pad(router_w[j], ((0, 0), (0, ROUTER_PAD - N_EXPERTS)))
            rb = jnp.pad(router_b[j][None], ((0, 0), (0, ROUTER_PAD - N_EXPERTS)), constant_values=NEG_BIG)
            x2d, h2, route = _mix(x2d, attn, ys, u, ga, gb, lp, tm, router=(rw, rb))
            tok_of_slot, tile_expert, n_used, slots = _routing(route, tm_ffn)
            n_tiles = tok_of_slot.shape[0] // tm_ffn
            n_calls = EXPERT_CALLS if n_tiles % EXPERT_CALLS == 0 else 1
            per_call = n_tiles // n_calls
            ysort = jnp.zeros((tok_of_slot.shape[0], d), F32)
            for c in range(n_calls):
                lo = c * per_call
                xs = jnp.take(h2, tok_of_slot[lo * tm_ffn:(lo + per_call) * tm_ffn], axis=0)
                ysort = _grouped_ffn(tile_expert[lo:lo + per_call], n_used - lo, xs, moe_w_gate[j],
                                     moe_w_up[j], moe_w_down[j], None, tm_ffn, 512, ybuf=ysort, tile_offset=lo)
            x2d = _combine(x2d, jnp.take(ysort, slots[:, 0], axis=0),
                           jnp.take(ysort, slots[:, 1], axis=0), route, tm)
    return x2d.reshape(batch, seq, d)
```

```python
import functools
import math

import jax
import jax.numpy as jnp
from jax import lax
from jax.experimental import pallas as pl
from jax.experimental.pallas import tpu as pltpu

EPS = 1e-6
N_HEADS = 8
QK_NOPE_DIM = 64
QK_ROPE_DIM = 32
QK_HEAD_DIM = QK_NOPE_DIM + QK_ROPE_DIM
V_HEAD_DIM = 64
LANES = 128
HEAD_PAD = LANES
MXU_DIM = 256
Q_LORA_RANK = 256
KV_LORA_RANK = 128
ROPE_THETA = 10000.0
S5_WIDTH = 512
S5_GROUP = 16
S5_GROUPS = S5_WIDTH // S5_GROUP
S5_STATE = 64
S5_CHUNK = 16
N_EXPERTS = 8
TOP_K = 2
ROUTER_PAD = LANES
EXPERT_CALLS = 4
NEG_BIG = -1e30
LOG2E = 1.4426950408889634

VMEM_LIMIT_BYTES = 56 * 1024 * 1024

F32 = jnp.float32
BF16 = jnp.bfloat16


def _dot(a, b):
    return jnp.dot(a, b, preferred_element_type=F32)


def _lane_tile(a, n):
    return jnp.concatenate([a] * n, axis=1)


def _rms_scale(x, ones_ref):
    ssq = _dot((x * x).astype(BF16), ones_ref[...])
    return lax.rsqrt(ssq * (1.0 / x.shape[1]) + EPS)


def _rms(x, g, ones_ref):
    return x * _lane_tile(_rms_scale(x, ones_ref), x.shape[1] // LANES) * g


def _params(*semantics):
    return pltpu.CompilerParams(dimension_semantics=semantics, vmem_limit_bytes=VMEM_LIMIT_BYTES)


def _full(shape):
    return pl.BlockSpec(shape, lambda *_: (0,) * len(shape))


def _inproj_kernel(x_ref, g_ref, onesd_ref, w1_ref, b1_ref, w2_ref, b2_ref, qng_ref, onesq_ref,
                   kvng_ref, oneskv_ref, wuq_ref, wuqr_ref, wk_ref, wv_ref, vones_ref, hones_ref,
                   qa_ref, qb_ref, ka_ref, kb_ref,
                   q_ref, k_ref, v_ref, u_ref, ga_ref, gb_ref):
    d = x_ref.shape[1]
    h = _rms(x_ref[...], g_ref[...], onesd_ref).astype(BF16)
    p1 = _dot(h, w1_ref[...]) + b1_ref[...]
    p2 = _dot(h, w2_ref[...]) + b2_ref[...]
    ga_ref[...] = jax.nn.sigmoid(p2[:, :d]).astype(BF16)
    gb_ref[...] = jax.nn.sigmoid(p2[:, d:]).astype(BF16)
    c0 = Q_LORA_RANK
    c1 = c0 + KV_LORA_RANK
    c2 = c1 + HEAD_PAD
    c3 = c2 + HEAD_PAD
    u_ref[...] = p1[:, c3:]
    qn = _rms(p1[:, :c0], qng_ref[...], onesq_ref).astype(BF16)
    kvn = _rms(p1[:, c0:c1], kvng_ref[...], oneskv_ref).astype(BF16)
    v_ref[...] = (_dot(kvn, wv_ref[...]) + vones_ref[...]).astype(BF16)

    def head_norm_rope(raw, rot, a_tab, b_tab):
        chunks = []
        for c in range(raw.shape[1] // MXU_DIM):
            blk = raw[:, c * MXU_DIM:(c + 1) * MXU_DIM]
            chunks.append(_dot((blk * blk).astype(BF16), hones_ref[...]))
        rs = lax.rsqrt(jnp.concatenate(chunks, axis=1) * (1.0 / QK_HEAD_DIM) + EPS)
        return rs * (raw * _lane_tile(a_tab, N_HEADS) + rot * _lane_tile(b_tab, N_HEADS))

    q_raw = _dot(qn, wuq_ref[...])
    q_rot = _dot(qn, wuqr_ref[...])
    q_ref[...] = head_norm_rope(q_raw, q_rot, qa_ref[...], qb_ref[...]).astype(BF16)
    k_raw = _dot(kvn, wk_ref[...]) + _lane_tile(p1[:, c1:c2], N_HEADS)
    k_rot = _lane_tile(p1[:, c2:c3], N_HEADS)
    k_ref[...] = head_norm_rope(k_raw, k_rot, ka_ref[...], kb_ref[...]).astype(BF16)


def _inproj(x2d, lp, seq, tm):
    t, d = x2d.shape
    nseq = seq // tm
    hp = N_HEADS * HEAD_PAD
    row = lambda i: (i, 0)
    pos = lambda i: (i % nseq, 0)
    consts = [lp[k] for k in ("ln_mix_g", "ones_d", "w1", "b1", "w2", "b2", "q_norm_g", "ones_q",
                              "kv_norm_g", "ones_kv", "wuq", "wuq_rot", "wk", "wv", "vones", "head_ones")]
    tabs = [lp[k] for k in ("qa", "qb", "ka", "kb")]
    in_specs = ([pl.BlockSpec((tm, d), row)] + [_full(c.shape) for c in consts]
                + [pl.BlockSpec((tm, HEAD_PAD), pos)] * 4)
    wide = lambda n, dt: (jax.ShapeDtypeStruct((t, n), dt), pl.BlockSpec((tm, n), row))
    outs = [wide(hp, BF16), wide(hp, BF16), wide(hp, BF16), wide(S5_WIDTH, F32), wide(d, BF16), wide(d, BF16)]
    return pl.pallas_call(
        _inproj_kernel, grid=(t // tm,), in_specs=in_specs,
        out_specs=tuple(o[1] for o in outs), out_shape=tuple(o[0] for o in outs),
        compiler_params=_params("parallel"), name="inproj",
    )(x2d, *consts, *tabs)


def _attn_kernel(q_ref, k_ref, v_ref, o_ref, m_scr, acc_scr):
    i = pl.program_id(1)
    j = pl.program_id(2)
    tq = q_ref.shape[0]
    tk = k_ref.shape[0]

    @pl.when(j == 0)
    def _():
        m_scr[...] = jnp.full(m_scr.shape, -jnp.inf, F32)
        acc_scr[...] = jnp.zeros(acc_scr.shape, F32)

    def step(masked):
        if masked:
            rows = lax.broadcasted_iota(jnp.int32, (tq, tk), 0)
            cols = lax.broadcasted_iota(jnp.int32, (tq, tk), 1)
            keep = cols <= rows
        for hd in range(N_HEADS):
            sl = slice(hd * HEAD_PAD, (hd + 1) * HEAD_PAD)
            s = lax.dot_general(q_ref[:, sl], k_ref[:, sl], (((1,), (1,)), ((), ())),
                                preferred_element_type=F32)
            if masked:
                s = jnp.where(keep, s, -jnp.inf)
            parts = [s[:, c * LANES:(c + 1) * LANES] for c in range(tk // LANES)]
            blk_max = functools.reduce(jnp.maximum, parts)
            m_prev = m_scr[hd]
            m_new = jnp.maximum(m_prev, jnp.max(blk_max, axis=-1, keepdims=True))
            alpha = jnp.exp2(m_prev - m_new)
            p = jnp.concatenate([jnp.exp2(part - m_new) for part in parts], axis=1).astype(BF16)
            acc_scr[hd] = alpha * acc_scr[hd] + _dot(p, v_ref[:, sl])
            m_scr[hd] = m_new

    @pl.when(j < i)
    def _():
        step(False)

    @pl.when(j == i)
    def _():
        step(True)
        lane = lax.broadcasted_iota(jnp.int32, (tq, HEAD_PAD), 1)
        for pr in range(N_HEADS // 2):
            a0 = acc_scr[2 * pr]
            a1 = acc_scr[2 * pr + 1]
            lo = a0 / pltpu.roll(a0, V_HEAD_DIM, 1)
            hi = pltpu.roll(a1, V_HEAD_DIM, 1) / a1
            o_ref[:, pr * HEAD_PAD:(pr + 1) * HEAD_PAD] = jnp.where(lane < V_HEAD_DIM, lo, hi).astype(BF16)


def _attention(q, k, v, batch, seq, tq):
    nq = seq // tq
    hp = N_HEADS * HEAD_PAD
    vw = N_HEADS * V_HEAD_DIM
    qmap = lambda b, i, j: (b * nq + i, 0)
    kmap = lambda b, i, j: (b * nq + jnp.minimum(j, i), 0)
    return pl.pallas_call(
        _attn_kernel, grid=(batch, nq, nq),
        in_specs=[pl.BlockSpec((tq, hp), qmap), pl.BlockSpec((tq, hp), kmap),
                  pl.BlockSpec((tq, hp), kmap)],
        out_specs=pl.BlockSpec((tq, vw), qmap),
        out_shape=jax.ShapeDtypeStruct((batch * seq, vw), BF16),
        scratch_shapes=[pltpu.VMEM((N_HEADS, tq, HEAD_PAD), F32), pltpu.VMEM((N_HEADS, tq, HEAD_PAD), F32)],
        compiler_params=_params("parallel", "parallel", "arbitrary"), name="attention",
    )(q, k, v)


def _s5_kernel(u_ref, kc_ref, pc_ref, qc_ref, a_ref, y_ref, acc_scr, tcat_scr, p_scr, q_scr):
    ln = S5_CHUNK
    chunks = acc_scr.shape[0]
    half = a_ref.shape[2] // 2
    gpb = LANES // S5_GROUP

    @pl.when(pl.program_id(1) == 0)
    def _():
        def diag_mask(shape, row_div, col_div, col_mod):
            r = lax.broadcasted_iota(jnp.int32, shape, 0) // row_div
            c = (lax.broadcasted_iota(jnp.int32, shape, 1) % col_mod) // col_div
            return r == c

        rep = lambda x: jnp.concatenate([x] * gpb, axis=0)
        mask_t = diag_mask((LANES, LANES), S5_GROUP, S5_GROUP, LANES)
        mask_p = diag_mask((LANES, 2 * half), S5_GROUP, S5_STATE, half)
        mask_q = diag_mask((half, LANES), S5_STATE, S5_GROUP, LANES)
        for d in range(ln):
            tcat_scr[:, d * LANES:(d + 1) * LANES] = jnp.where(mask_t, rep(kc_ref[0, d]), 0.0).astype(BF16)
            p_scr[d] = jnp.where(mask_p, rep(pc_ref[0, d]), 0.0).astype(BF16)
            qc = qc_ref[0, d]
            q_scr[:half, d * LANES:(d + 1) * LANES] = jnp.where(mask_q, rep(qc[:S5_STATE]), 0.0).astype(BF16)
            q_scr[half:, d * LANES:(d + 1) * LANES] = jnp.where(mask_q, rep(qc[S5_STATE:]), 0.0).astype(BF16)

    z = jnp.zeros((chunks, 2 * half), F32)
    for s in range(ln):
        us = u_ref[pl.ds(s, chunks, stride=ln), :].astype(BF16)
        contrib = _dot(us, tcat_scr[:, :(ln - s) * LANES])
        if s == 0:
            acc_scr[...] = contrib
        else:
            acc_scr[:, s * LANES:] += contrib
        z = z + _dot(us, p_scr[s])
    row = lax.broadcasted_iota(jnp.int32, z.shape, 0)
    a = a_ref[0]
    off = 1
    lvl = 0
    while off < chunks:
        prev = pltpu.roll(z, off, 0)
        contrib = prev * a[2 * lvl:2 * lvl + 1] + pltpu.roll(prev, half, 1) * a[2 * lvl + 1:2 * lvl + 2]
        z = z + jnp.where(row >= off, contrib, 0.0)
        off *= 2
        lvl += 1
    x_start = jnp.where(row >= 1, pltpu.roll(z, 1, 0), 0.0)
    y = acc_scr[...] + _dot(x_start.astype(BF16), q_scr[...])
    for t in range(ln):
        y_ref[pl.ds(t, chunks, stride=ln), :] = y[:, t * LANES:(t + 1) * LANES]


def _s5_scan(u, mats, batch, seq):
    t, width = u.shape
    kc, pc, qc, a = mats
    nblk = width // LANES
    state_w = a.shape[2]
    blk = lambda arr: pl.BlockSpec((1,) + arr.shape[1:], lambda g, b: (g,) + (0,) * (arr.ndim - 1))
    act = pl.BlockSpec((seq, LANES), lambda g, b: (b, g))
    return pl.pallas_call(
        _s5_kernel, grid=(nblk, batch),
        in_specs=[act, blk(kc), blk(pc), blk(qc), blk(a)],
        out_specs=act, out_shape=jax.ShapeDtypeStruct((t, width), F32),
        scratch_shapes=[pltpu.VMEM((seq // S5_CHUNK, S5_CHUNK * LANES), F32),
                        pltpu.VMEM((LANES, S5_CHUNK * LANES), BF16),
                        pltpu.VMEM((S5_CHUNK, LANES, state_w), BF16),
                        pltpu.VMEM((state_w, S5_CHUNK * LANES), BF16)],
        compiler_params=_params("parallel", "arbitrary"), name="s5_scan",
    )(u, kc, pc, qc, a)


def _s5_matrices(lam_re, lam_im, log_step, b_re, b_im, c_re, c_im, chunks_per_seq):
    hi = lax.Precision.HIGHEST
    ln = S5_CHUNK
    gpb = LANES // S5_GROUP
    nblk = S5_GROUPS // gpb
    step = jnp.exp(log_step)[:, None]
    ar = lam_re * step
    ai = lam_im * step
    lbr = jnp.exp(ar) * jnp.cos(ai)
    lbi = jnp.exp(ar) * jnp.sin(ai)
    den = lam_re * lam_re + lam_im * lam_im
    cr = ((lbr - 1.0) * lam_re + lbi * lam_im) / den
    ci = (lbi * lam_re - (lbr - 1.0) * lam_im) / den
    bbr = cr[..., None] * b_re - ci[..., None] * b_im
    bbi = cr[..., None] * b_im + ci[..., None] * b_re
    dd = jnp.arange(ln + 1, dtype=F32)[:, None, None]
    pr = jnp.exp(dd * ar) * jnp.cos(dd * ai)
    pi = jnp.exp(dd * ar) * jnp.sin(dd * ai)
    tr = pr[:ln, :, :, None] * bbr - pi[:ln, :, :, None] * bbi
    ti = pr[:ln, :, :, None] * bbi + pi[:ln, :, :, None] * bbr
    kd = (jnp.einsum("gcp,dgpe->dgce", c_re, tr, precision=hi)
          - jnp.einsum("gcp,dgpe->dgce", c_im, ti, precision=hi))
    def compact(x):
        r, c = x.shape[-2:]
        return x.reshape(ln, nblk, gpb, r, c).transpose(1, 0, 4, 2, 3).reshape(nblk, ln, c, gpb * r)

    kc = compact(kd)
    pc = jnp.concatenate([compact(tr[::-1]), compact(ti[::-1])], axis=-1)
    q_re = c_re[None] * pr[1:, :, None, :] - c_im[None] * pi[1:, :, None, :]
    q_im = -(c_re[None] * pi[1:, :, None, :] + c_im[None] * pr[1:, :, None, :])
    qc = jnp.concatenate([compact(q_re), compact(q_im)], axis=2)
    levels = int(math.log2(chunks_per_seq))
    e = (ln * (2.0 ** jnp.arange(levels, dtype=F32)))[:, None, None]
    er = (jnp.exp(e * ar) * jnp.cos(e * ai)).reshape(levels, nblk, gpb * S5_STATE)
    ei = (jnp.exp(e * ar) * jnp.sin(e * ai)).reshape(levels, nblk, gpb * S5_STATE)
    a = jnp.stack([jnp.concatenate([er, er], -1), jnp.concatenate([-ei, ei], -1)], axis=1)
    a = a.transpose(2, 0, 1, 3).reshape(nblk, 2 * levels, 2 * gpb * S5_STATE)
    return kc, pc, qc, a


def _split_bf16(a):
    hi = a.astype(BF16)
    return hi, (a - hi.astype(F32)).astype(BF16)


def _mix_kernel(*refs, routed):
    (x_ref, attn_ref, ys_ref, u_ref, ga_ref, gb_ref, d_ref, wglu_ref, bglu_ref, wpa_ref, wps_ref,
     wout_ref, lng_ref, onesd_ref) = refs[:14]
    if routed:
        rw_ref, rb_ref, xo_ref, h_ref, route_ref = refs[14:]
    else:
        xo_ref, h_ref = refs[14:]
    y = ys_ref[...] + d_ref[...] * u_ref[...]
    y = jax.nn.gelu(y, approximate=True)
    ssm = y * jax.nn.sigmoid(_dot(y.astype(BF16), wglu_ref[...]) + bglu_ref[...])
    pa = _dot(attn_ref[...], wpa_ref[...])
    ps = _dot(ssm.astype(BF16), wps_ref[...])
    mix = ga_ref[...].astype(F32) * pa + gb_ref[...].astype(F32) * ps
    xn = x_ref[...] + _dot(mix.astype(BF16), wout_ref[...])
    xo_ref[...] = xn
    h2 = _rms(xn, lng_ref[...], onesd_ref)
    if not routed:
        h_ref[...] = h2.astype(BF16)
        return
    h_ref[...] = h2
    h_hi, h_lo = _split_bf16(h2)
    r_hi, r_lo = _split_bf16(rw_ref[...])
    logits = _dot(h_hi, r_hi) + _dot(h_lo, r_hi) + _dot(h_hi, r_lo) + rb_ref[...]
    lane = lax.broadcasted_iota(jnp.int32, logits.shape, 1).astype(F32)
    m1 = jnp.max(logits, axis=-1, keepdims=True)
    i1 = jnp.min(jnp.where(logits == m1, lane, float(ROUTER_PAD)), axis=-1, keepdims=True)
    rest = jnp.where(lane == i1, NEG_BIG, logits)
    m2 = jnp.max(rest, axis=-1, keepdims=True)
    i2 = jnp.min(jnp.where(rest == m2, lane, float(ROUTER_PAD)), axis=-1, keepdims=True)
    e2 = jnp.exp(m2 - m1)
    w1 = 1.0 / (1.0 + e2)
    route_ref[...] = jnp.where(lane == 0.0, i1, jnp.where(lane == 1.0, i2, jnp.where(
        lane == 2.0, w1, jnp.where(lane == 3.0, e2 * w1, 0.0))))


def _mix(x2d, attn, ys, u, ga, gb, lp, tm, router=None):
    t, d = x2d.shape
    row = lambda i: (i, 0)
    routed = router is not None
    consts = [lp[k] for k in ("s5_d", "w_glu", "b_glu", "w_proj_attn", "w_proj_ssm", "w_out",
                              "ln_ffn_g", "ones_d")]
    if routed:
        consts += list(router)
    acts = [x2d, attn, ys, u, ga, gb]
    in_specs = [pl.BlockSpec((tm, a.shape[1]), row) for a in acts] + [_full(c.shape) for c in consts]
    if routed:
        out_shape = [jax.ShapeDtypeStruct((t, d), F32), jax.ShapeDtypeStruct((t, d), F32),
                     jax.ShapeDtypeStruct((t, ROUTER_PAD), F32)]
    else:
        out_shape = [jax.ShapeDtypeStruct((t, d), F32), jax.ShapeDtypeStruct((t, d), BF16)]
    out_specs = [pl.BlockSpec((tm, o.shape[1]), row) for o in out_shape]
    return pl.pallas_call(
        functools.partial(_mix_kernel, routed=routed), grid=(t // tm,), in_specs=in_specs,
        out_specs=tuple(out_specs), out_shape=tuple(out_shape),
        compiler_params=_params("parallel"), name="mix_routed" if routed else "mix",
    )(*acts, *consts)


def _ffn_kernel(te_ref, nu_ref, *refs, routed):
    if routed:
        x_ref, wg_ref, wu_ref, wd_ref, _, o_ref, acc_scr, x_scr = refs
    else:
        x_ref, wg_ref, wu_ref, wd_ref, res_ref, o_ref, acc_scr = refs
    i = pl.program_id(0)
    j = pl.program_id(1)
    last = pl.num_programs(1) - 1
    used = i < nu_ref[0]

    if routed:
        @pl.when(jnp.logical_and(used, j == 0))
        def _():
            x_scr[...] = x_ref[...].astype(BF16)

    @pl.when(used)
    def _():
        x = x_scr[...] if routed else x_ref[...]
        g = _dot(x, wg_ref[0].astype(BF16))
        a = (g * jax.nn.sigmoid(g) * _dot(x, wu_ref[0].astype(BF16))).astype(BF16)
        part = _dot(a, wd_ref[0].astype(BF16))

        @pl.when(j == 0)
        def _():
            acc_scr[...] = part

        @pl.when(j > 0)
        def _():
            acc_scr[...] += part

    @pl.when(jnp.logical_and(used, j == last))
    def _():
        if routed:
            o_ref[...] = acc_scr[...]
        else:
            o_ref[...] = acc_scr[...] + res_ref[...]

    @pl.when(jnp.logical_and(jnp.logical_not(used), j == last))
    def _():
        o_ref[...] = jnp.zeros(o_ref.shape, o_ref.dtype)


def _grouped_ffn(tile_expert, n_used, xs, wg, wu, wd, res, tm, tf, ybuf=None, tile_offset=0):
    routed = res is None
    n = xs.shape[0]
    d = wg.shape[1]
    nff = wg.shape[2] // tf

    def jeff(i, j, nu):
        return jnp.where(i < nu[0], j, nff - 1)

    row = lambda i, j, te, nu: (i, 0)
    in_specs = [
        pl.BlockSpec((tm, xs.shape[1]), row),
        pl.BlockSpec((1, d, tf), lambda i, j, te, nu: (te[i], 0, jeff(i, j, nu))),
        pl.BlockSpec((1, d, tf), lambda i, j, te, nu: (te[i], 0, jeff(i, j, nu))),
        pl.BlockSpec((1, tf, d), lambda i, j, te, nu: (te[i], jeff(i, j, nu), 0)),
    ]
    args = [xs, wg, wu, wd]
    scratch = [pltpu.VMEM((tm, d), F32)]
    aliases = {}
    if routed:
        scratch.append(pltpu.VMEM((tm, d), BF16))
        in_specs.append(pl.BlockSpec(memory_space=pl.ANY))
        args.append(ybuf)
        aliases = {2 + len(args) - 1: 0}
        out_rows = ybuf.shape[0]
    else:
        in_specs.append(pl.BlockSpec((tm, d), row))
        args.append(res)
        out_rows = n
    grid_spec = pltpu.PrefetchScalarGridSpec(
        num_scalar_prefetch=2, grid=(n // tm, nff), in_specs=in_specs,
        out_specs=pl.BlockSpec((tm, d), lambda i, j, te, nu: (i + tile_offset, 0)), scratch_shapes=scratch)
    return pl.pallas_call(
        functools.partial(_ffn_kernel, routed=routed), grid_spec=grid_spec,
        out_shape=jax.ShapeDtypeStruct((out_rows, d), F32), input_output_aliases=aliases,
        compiler_params=_params("parallel", "arbitrary"),
        name="ffn_experts" if routed else "ffn_dense",
    )(tile_expert, n_used, *args)


def _combine_kernel(x_ref, y0_ref, y1_ref, route_ref, o_ref):
    r = route_ref[...]
    w0 = r[:, 2:3]
    w1 = r[:, 3:4]
    o_ref[...] = x_ref[...] + w0 * y0_ref[...] + w1 * y1_ref[...]


def _combine(x2d, y0, y1, route, tm):
    t, d = x2d.shape
    row = lambda i: (i, 0)
    return pl.pallas_call(
        _combine_kernel, grid=(t // tm,),
        in_specs=[pl.BlockSpec((tm, d), row), pl.BlockSpec((tm, d), row),
                  pl.BlockSpec((tm, d), row), pl.BlockSpec((tm, ROUTER_PAD), row)],
        out_specs=pl.BlockSpec((tm, d), row), out_shape=jax.ShapeDtypeStruct((t, d), F32),
        compiler_params=_params("parallel"), name="combine",
    )(x2d, y0, y1, route)


def _rope_tables(seq):
    half = QK_ROPE_DIM // 2
    pos = jnp.arange(seq, dtype=F32)
    inv = jnp.power(ROPE_THETA, -jnp.arange(0, QK_ROPE_DIM, 2, dtype=F32) / QK_ROPE_DIM)
    ang = pos[:, None] * inv[None, :]
    cos, sin = jnp.cos(ang), jnp.sin(ang)
    tail = jnp.zeros((seq, HEAD_PAD - QK_HEAD_DIM), F32)
    c = jnp.concatenate([jnp.ones((seq, QK_NOPE_DIM), F32), cos, cos, tail], axis=1)
    s = jnp.concatenate([jnp.zeros((seq, QK_NOPE_DIM), F32), -sin, sin, tail], axis=1)
    return c, s


def _rope_partner(a):
    half = QK_ROPE_DIM // 2
    lead = a.shape[:-1]
    a = a.reshape(lead + (-1, HEAD_PAD))
    x1 = a[..., QK_NOPE_DIM:QK_NOPE_DIM + half]
    x2 = a[..., QK_NOPE_DIM + half:QK_HEAD_DIM]
    out = jnp.concatenate([jnp.zeros_like(a[..., :QK_NOPE_DIM]), x2, x1,
                           jnp.zeros_like(a[..., QK_HEAD_DIM:])], axis=-1)
    return out.reshape(lead + (-1,))


def _pad_heads(w, per_head, place_at=0):
    k = w.shape[0]
    w = w.reshape(k, N_HEADS, per_head)
    w = jnp.pad(w, ((0, 0), (0, 0), (place_at, HEAD_PAD - per_head - place_at)))
    return w.reshape(k, N_HEADS * HEAD_PAD)


def _layer_params(l, p, rope):
    d = p["w_in"].shape[1]
    w_in, b_in = p["w_in"][l], p["b_in"][l]
    o = [0, Q_LORA_RANK, Q_LORA_RANK + KV_LORA_RANK, Q_LORA_RANK + KV_LORA_RANK + QK_ROPE_DIM]
    o.append(o[3] + S5_WIDTH)
    o.append(o[4] + d)
    pad_rope = lambda a: jnp.pad(a, ((0, 0), (QK_NOPE_DIM, HEAD_PAD - QK_HEAD_DIM)))
    w_kr, b_kr = pad_rope(w_in[:, o[2]:o[3]]), pad_rope(b_in[None, o[2]:o[3]])
    w1 = jnp.concatenate([w_in[:, o[0]:o[2]], w_kr, _rope_partner(w_kr), w_in[:, o[3]:o[4]]], axis=1)
    b1 = jnp.concatenate([b_in[None, o[0]:o[2]], b_kr, _rope_partner(b_kr), b_in[None, o[3]:o[4]]], axis=1)
    w_ukv = p["w_ukv"][l].reshape(KV_LORA_RANK, N_HEADS, QK_NOPE_DIM + V_HEAD_DIM)
    wuq = _pad_heads(p["w_uq"][l], QK_HEAD_DIM)
    pad_g = lambda g: jnp.pad(g[None, :], ((0, 0), (0, HEAD_PAD - QK_HEAD_DIM)))
    qg, kg = pad_g(p["q_head_g"][l]), pad_g(p["k_head_g"][l])
    cos, sin = rope
    qscale = QK_HEAD_DIM ** -0.5 * LOG2E
    head_ones = jnp.kron(jnp.eye(MXU_DIM // HEAD_PAD, dtype=F32), jnp.ones((HEAD_PAD, HEAD_PAD), F32))
    return {
        "ln_mix_g": p["ln_mix_g"][l][None], "ones_d": jnp.ones((d, LANES), BF16),
        "w1": w1.astype(BF16), "b1": b1, "w2": w_in[:, o[4]:].astype(BF16), "b2": b_in[None, o[4]:],
        "q_norm_g": p["q_norm_g"][l][None], "ones_q": jnp.ones((Q_LORA_RANK, LANES), BF16),
        "kv_norm_g": p["kv_norm_g"][l][None], "ones_kv": jnp.ones((KV_LORA_RANK, LANES), BF16),
        "wuq": wuq.astype(BF16), "wuq_rot": _rope_partner(wuq).astype(BF16),
        "wk": _pad_heads(w_ukv[:, :, :QK_NOPE_DIM].reshape(KV_LORA_RANK, -1), QK_NOPE_DIM).astype(BF16),
        "wv": _pad_heads(w_ukv[:, :, QK_NOPE_DIM:].reshape(KV_LORA_RANK, -1), V_HEAD_DIM).astype(BF16),
        "vones": _pad_heads(jnp.ones((1, N_HEADS * (HEAD_PAD - V_HEAD_DIM)), F32),
                            HEAD_PAD - V_HEAD_DIM, place_at=V_HEAD_DIM),
        "head_ones": head_ones.astype(BF16),
        "qa": qg * cos * qscale, "qb": _rope_partner(qg) * sin * qscale,
        "ka": kg * cos, "kb": _rope_partner(kg) * sin,
        "s5_d": p["s5_d"][l][None], "w_glu": p["w_glu"][l].astype(BF16), "b_glu": p["b_glu"][l][None],
        "w_proj_attn": p["w_proj_attn"][l].astype(BF16), "w_proj_ssm": p["w_proj_ssm"][l].astype(BF16),
        "w_out": p["w_out"][l].astype(BF16), "ln_ffn_g": p["ln_ffn_g"][l][None],
    }


def _routing(route, tm):
    t = route.shape[0]
    e = route[:, :TOP_K].astype(jnp.int32)
    seli = jnp.sum((e[:, :, None] == jnp.arange(N_EXPERTS)[None, None, :]).astype(jnp.int32), axis=1)
    cnt = jnp.sum(seli, axis=0)
    padded = ((cnt + tm - 1) // tm) * tm
    ends = jnp.cumsum(padded)
    pos_te = (ends - padded)[None, :] + jnp.cumsum(seli, axis=0) - seli
    slots = jnp.take_along_axis(pos_te, e, axis=1)
    n_slots = TOP_K * t + N_EXPERTS * tm
    tok = jnp.broadcast_to(jnp.arange(t, dtype=jnp.int32)[:, None], (t, TOP_K))
    tok_of_slot = jnp.zeros((n_slots,), jnp.int32).at[slots.reshape(-1)].set(
        tok.reshape(-1), unique_indices=True)
    tile_start = jnp.arange(n_slots // tm, dtype=jnp.int32) * tm
    tile_expert = jnp.sum((tile_start[:, None] >= ends[None, :]).astype(jnp.int32), axis=1)
    tile_expert = jnp.minimum(tile_expert, N_EXPERTS - 1)
    n_used = (ends[-1] // tm).astype(jnp.int32)[None]
    tile_expert = jnp.where(tile_start < ends[-1], tile_expert, tile_expert[jnp.maximum(n_used[0] - 1, 0)])
    return tok_of_slot, tile_expert.astype(jnp.int32), n_used, slots


def kernel(x, ln_mix_g, w_in, b_in, q_norm_g, w_uq, kv_norm_g, w_ukv, q_head_g, k_head_g,
           s5_lam_re, s5_lam_im, s5_log_step, s5_b_re, s5_b_im, s5_c_re, s5_c_im, s5_d,
           w_glu, b_glu, w_proj_attn, w_proj_ssm, w_out, ln_ffn_g,
           ffn_w_gate, ffn_w_up, ffn_w_down, router_w, router_b,
           moe_w_gate, moe_w_up, moe_w_down):
    batch, seq, d = x.shape
    depth = w_in.shape[0]
    t = batch * seq
    tm = min(512, seq)
    tq = min(512, seq)
    tm_ffn = min(1024, t)
    chunks_per_seq = seq // S5_CHUNK
    assert chunks_per_seq & (chunks_per_seq - 1) == 0 and seq % tm == 0 and t % tm_ffn == 0
    p = dict(ln_mix_g=ln_mix_g, w_in=w_in, b_in=b_in, q_norm_g=q_norm_g, w_uq=w_uq,
             kv_norm_g=kv_norm_g, w_ukv=w_ukv, q_head_g=q_head_g, k_head_g=k_head_g, s5_d=s5_d,
             w_glu=w_glu, b_glu=b_glu, w_proj_attn=w_proj_attn, w_proj_ssm=w_proj_ssm,
             w_out=w_out, ln_ffn_g=ln_ffn_g)
    rope = _rope_tables(seq)
    x2d = x.reshape(t, d)
    for l in range(depth):
        lp = _layer_params(l, p, rope)
        q, k, v, u, ga, gb = _inproj(x2d, lp, seq, tm)
        attn = _attention(q, k, v, batch, seq, tq)
        mats = _s5_matrices(s5_lam_re[l], s5_lam_im[l], s5_log_step[l], s5_b_re[l], s5_b_im[l],
                            s5_c_re[l], s5_c_im[l], chunks_per_seq)
        ys = _s5_scan(u, mats, batch, seq)
        j = l // 2
        if l % 2 == 0:
            x2d, h2 = _mix(x2d, attn, ys, u, ga, gb, lp, tm)
            n_tiles = t // tm_ffn
            x2d = _grouped_ffn(
                jnp.zeros((n_tiles,), jnp.int32), jnp.full((1,), n_tiles, jnp.int32), h2,
                ffn_w_gate[j][None], ffn_w_up[j][None], ffn_w_down[j][None], x2d, tm_ffn, 256)
        else:
            rw = jnp.pad(router_w[j], ((0, 0), (0, ROUTER_PAD - N_EXPERTS)))
            rb = jnp.pad(router_b[j][None], ((0, 0), (0, ROUTER_PAD - N_EXPERTS)), constant_values=NEG_BIG)
            x2d, h2, route = _mix(x2d, attn, ys, u, ga, gb, lp, tm, router=(rw, rb))
            tok_of_slot, tile_expert, n_used, slots = _routing(route, tm_ffn)
            n_tiles = tok_of_slot.shape[0] // tm_ffn
            n_calls = EXPERT_CALLS if n_tiles % EXPERT_CALLS == 0 else 1
            per_call = n_tiles // n_calls
            ysort = jnp.zeros((tok_of_slot.shape[0], d), F32)
            for c in range(n_calls):
                lo = c * per_call
                xs = jnp.take(h2, tok_of_slot[lo * tm_ffn:(lo + per_call) * tm_ffn], axis=0)
                ysort = _grouped_ffn(tile_expert[lo:lo + per_call], n_used - lo, xs, moe_w_gate[j],
                                     moe_w_up[j], moe_w_down[j], None, tm_ffn, 512, ybuf=ysort, tile_offset=lo)
            x2d = _combine(x2d, jnp.take(ysort, slots[:, 0], axis=0),
                           jnp.take(ysort, slots[:, 1], axis=0), route, tm)
    return x2d.reshape(batch, seq, d)
```

```python
import functools
import math

import jax
import jax.numpy as jnp
from jax import lax
from jax.experimental import pallas as pl
from jax.experimental.pallas import tpu as pltpu

EPS = 1e-6
N_HEADS = 8
QK_NOPE_DIM = 64
QK_ROPE_DIM = 32
QK_HEAD_DIM = QK_NOPE_DIM + QK_ROPE_DIM
V_HEAD_DIM = 64
LANES = 128
SUBLANES = 8
HEAD_PAD = LANES
MXU_DIM = 256
Q_LORA_RANK = 256
KV_LORA_RANK = 128
ROPE_THETA = 10000.0
S5_WIDTH = 512
S5_GROUP = 16
S5_GROUPS = S5_WIDTH // S5_GROUP
S5_STATE = 64
S5_CHUNK = 16
N_EXPERTS = 8
TOP_K = 2
ROUTER_PAD = LANES
EXPERT_CALLS = 4
NEG_BIG = -1e30
LOG2E = 1.4426950408889634

VMEM_LIMIT_BYTES = 56 * 1024 * 1024

F32 = jnp.float32
BF16 = jnp.bfloat16


def _dot(a, b):
    return jnp.dot(a, b, preferred_element_type=F32)


def _lane_tile(a, n):
    return jnp.concatenate([a] * n, axis=1)


def _rms_scale(x, ones_ref):
    ssq = _dot((x * x).astype(BF16), ones_ref[...])
    return lax.rsqrt(ssq * (1.0 / x.shape[1]) + EPS)


def _rms(x, g, ones_ref):
    return x * _lane_tile(_rms_scale(x, ones_ref), x.shape[1] // LANES) * g


def _params(*semantics):
    return pltpu.CompilerParams(dimension_semantics=semantics, vmem_limit_bytes=VMEM_LIMIT_BYTES)


def _full(shape):
    return pl.BlockSpec(shape, lambda *_: (0,) * len(shape))


def _inproj_kernel(x_ref, g_ref, onesd_ref, w1_ref, b1_ref, w2_ref, b2_ref, qng_ref, onesq_ref,
                   kvng_ref, oneskv_ref, wuq_ref, wuqr_ref, wk_ref, wv_ref, vones_ref, hones_ref,
                   qa_ref, qb_ref, ka_ref, kb_ref,
                   q_ref, k_ref, v_ref, u_ref, ga_ref, gb_ref):
    d = x_ref.shape[1]
    h = _rms(x_ref[...], g_ref[...], onesd_ref).astype(BF16)
    p1 = _dot(h, w1_ref[...]) + b1_ref[...]
    p2 = _dot(h, w2_ref[...]) + b2_ref[...]
    ga_ref[...] = jax.nn.sigmoid(p2[:, :d]).astype(BF16)
    gb_ref[...] = jax.nn.sigmoid(p2[:, d:]).astype(BF16)
    c0 = Q_LORA_RANK
    c1 = c0 + KV_LORA_RANK
    c2 = c1 + HEAD_PAD
    c3 = c2 + HEAD_PAD
    u_ref[...] = p1[:, c3:]
    qn = _rms(p1[:, :c0], qng_ref[...], onesq_ref).astype(BF16)
    kvn = _rms(p1[:, c0:c1], kvng_ref[...], oneskv_ref).astype(BF16)
    v_ref[...] = (_dot(kvn, wv_ref[...]) + vones_ref[...]).astype(BF16)

    def head_norm_rope(raw, rot, a_tab, b_tab):
        chunks = []
        for c in range(raw.shape[1] // MXU_DIM):
            blk = raw[:, c * MXU_DIM:(c + 1) * MXU_DIM]
            chunks.append(_dot((blk * blk).astype(BF16), hones_ref[...]))
        rs = lax.rsqrt(jnp.concatenate(chunks, axis=1) * (1.0 / QK_HEAD_DIM) + EPS)
        return rs * (raw * _lane_tile(a_tab, N_HEADS) + rot * _lane_tile(b_tab, N_HEADS))

    q_raw = _dot(qn, wuq_ref[...])
    q_rot = _dot(qn, wuqr_ref[...])
    q_ref[...] = head_norm_rope(q_raw, q_rot, qa_ref[...], qb_ref[...]).astype(BF16)
    k_raw = _dot(kvn, wk_ref[...]) + _lane_tile(p1[:, c1:c2], N_HEADS)
    k_rot = _lane_tile(p1[:, c2:c3], N_HEADS)
    k_ref[...] = head_norm_rope(k_raw, k_rot, ka_ref[...], kb_ref[...]).astype(BF16)


def _inproj(x2d, lp, seq, tm):
    t, d = x2d.shape
    nseq = seq // tm
    hp = N_HEADS * HEAD_PAD
    row = lambda i: (i, 0)
    pos = lambda i: (i % nseq, 0)
    consts = [lp[k] for k in ("ln_mix_g", "ones_d", "w1", "b1", "w2", "b2", "q_norm_g", "ones_q",
                              "kv_norm_g", "ones_kv", "wuq", "wuq_rot", "wk", "wv", "vones", "head_ones")]
    tabs = [lp[k] for k in ("qa", "qb", "ka", "kb")]
    in_specs = ([pl.BlockSpec((tm, d), row)] + [_full(c.shape) for c in consts]
                + [pl.BlockSpec((tm, HEAD_PAD), pos)] * 4)
    wide = lambda n, dt: (jax.ShapeDtypeStruct((t, n), dt), pl.BlockSpec((tm, n), row))
    outs = [wide(hp, BF16), wide(hp, BF16), wide(hp, BF16), wide(S5_WIDTH, F32), wide(d, BF16), wide(d, BF16)]
    return pl.pallas_call(
        _inproj_kernel, grid=(t // tm,), in_specs=in_specs,
        out_specs=tuple(o[1] for o in outs), out_shape=tuple(o[0] for o in outs),
        compiler_params=_params("parallel"), name="inproj",
    )(x2d, *consts, *tabs)


def _attn_kernel(q_ref, k_ref, v_ref, o_ref, m_scr, acc_scr):
    i = pl.program_id(1)
    j = pl.program_id(2)
    tq = q_ref.shape[0]
    tk = k_ref.shape[0]

    @pl.when(j == 0)
    def _():
        m_scr[...] = jnp.full(m_scr.shape, -jnp.inf, F32)
        acc_scr[...] = jnp.zeros(acc_scr.shape, F32)

    def step(masked):
        if masked:
            rows = lax.broadcasted_iota(jnp.int32, (tq, tk), 0)
            cols = lax.broadcasted_iota(jnp.int32, (tq, tk), 1)
            keep = cols <= rows
        for hd in range(N_HEADS):
            sl = slice(hd * HEAD_PAD, (hd + 1) * HEAD_PAD)
            s = lax.dot_general(q_ref[:, sl], k_ref[:, sl], (((1,), (1,)), ((), ())),
                                preferred_element_type=F32)
            if masked:
                s = jnp.where(keep, s, -jnp.inf)
            parts = [s[:, c * LANES:(c + 1) * LANES] for c in range(tk // LANES)]
            blk_max = functools.reduce(jnp.maximum, parts)
            m_prev = m_scr[hd]
            m_new = jnp.maximum(m_prev, jnp.max(blk_max, axis=-1, keepdims=True))
            alpha = jnp.exp2(m_prev - m_new)
            p = jnp.concatenate([jnp.exp2(part - m_new) for part in parts], axis=1).astype(BF16)
            acc_scr[hd] = alpha * acc_scr[hd] + _dot(p, v_ref[:, sl])
            m_scr[hd] = m_new

    @pl.when(j < i)
    def _():
        step(False)

    @pl.when(j == i)
    def _():
        step(True)
        lane = lax.broadcasted_iota(jnp.int32, (tq, HEAD_PAD), 1)
        for pr in range(N_HEADS // 2):
            a0 = acc_scr[2 * pr]
            a1 = acc_scr[2 * pr + 1]
            lo = a0 / pltpu.roll(a0, V_HEAD_DIM, 1)
            hi = pltpu.roll(a1, V_HEAD_DIM, 1) / a1
            o_ref[:, pr * HEAD_PAD:(pr + 1) * HEAD_PAD] = jnp.where(lane < V_HEAD_DIM, lo, hi).astype(BF16)


def _attention(q, k, v, batch, seq, tq):
    nq = seq // tq
    hp = N_HEADS * HEAD_PAD
    vw = N_HEADS * V_HEAD_DIM
    qmap = lambda b, i, j: (b * nq + i, 0)
    kmap = lambda b, i, j: (b * nq + jnp.minimum(j, i), 0)
    return pl.pallas_call(
        _attn_kernel, grid=(batch, nq, nq),
        in_specs=[pl.BlockSpec((tq, hp), qmap), pl.BlockSpec((tq, hp), kmap),
                  pl.BlockSpec((tq, hp), kmap)],
        out_specs=pl.BlockSpec((tq, vw), qmap),
        out_shape=jax.ShapeDtypeStruct((batch * seq, vw), BF16),
        scratch_shapes=[pltpu.VMEM((N_HEADS, tq, HEAD_PAD), F32), pltpu.VMEM((N_HEADS, tq, HEAD_PAD), F32)],
        compiler_params=_params("parallel", "parallel", "arbitrary"), name="attention",
    )(q, k, v)


def _s5_kernel(u_ref, kc_ref, pc_ref, qc_ref, a_ref, y_ref, acc_scr, tcat_scr, p_scr, q_scr):
    ln = S5_CHUNK
    chunks = acc_scr.shape[0]
    half = a_ref.shape[2] // 2
    gpb = LANES // S5_GROUP

    @pl.when(pl.program_id(1) == 0)
    def _():
        def diag_mask(shape, row_div, col_div, col_mod):
            r = lax.broadcasted_iota(jnp.int32, shape, 0) // row_div
            c = (lax.broadcasted_iota(jnp.int32, shape, 1) % col_mod) // col_div
            return r == c

        rep = lambda x: jnp.concatenate([x] * gpb, axis=0)
        mask_t = diag_mask((LANES, LANES), S5_GROUP, S5_GROUP, LANES)
        mask_p = diag_mask((LANES, 2 * half), S5_GROUP, S5_STATE, half)
        mask_q = diag_mask((half, LANES), S5_STATE, S5_GROUP, LANES)
        for d in range(ln):
            tcat_scr[:, d * LANES:(d + 1) * LANES] = jnp.where(mask_t, rep(kc_ref[0, d]), 0.0).astype(BF16)
            p_scr[d] = jnp.where(mask_p, rep(pc_ref[0, d]), 0.0).astype(BF16)
            qc = qc_ref[0, d]
            q_scr[:half, d * LANES:(d + 1) * LANES] = jnp.where(mask_q, rep(qc[:S5_STATE]), 0.0).astype(BF16)
            q_scr[half:, d * LANES:(d + 1) * LANES] = jnp.where(mask_q, rep(qc[S5_STATE:]), 0.0).astype(BF16)

    z = jnp.zeros((chunks, 2 * half), F32)
    for s in range(ln):
        us = u_ref[pl.ds(s, chunks, stride=ln), :].astype(BF16)
        contrib = _dot(us, tcat_scr[:, :(ln - s) * LANES])
        if s == 0:
            acc_scr[...] = contrib
        else:
            acc_scr[:, s * LANES:] += contrib
        z = z + _dot(us, p_scr[s])
    row = lax.broadcasted_iota(jnp.int32, z.shape, 0)
    a = a_ref[0]
    off = 1
    lvl = 0
    while off < chunks:
        prev = pltpu.roll(z, off, 0)
        contrib = prev * a[2 * lvl:2 * lvl + 1] + pltpu.roll(prev, half, 1) * a[2 * lvl + 1:2 * lvl + 2]
        z = z + jnp.where(row >= off, contrib, 0.0)
        off *= 2
        lvl += 1
    x_start = jnp.where(row >= 1, pltpu.roll(z, 1, 0), 0.0)
    y = acc_scr[...] + _dot(x_start.astype(BF16), q_scr[...])
    for t in range(ln):
        y_ref[pl.ds(t, chunks, stride=ln), :] = y[:, t * LANES:(t + 1) * LANES]


def _s5_scan(u, mats, batch, seq):
    t, width = u.shape
    kc, pc, qc, a = mats
    nblk = width // LANES
    state_w = a.shape[2]
    blk = lambda arr: pl.BlockSpec((1,) + arr.shape[1:], lambda g, b: (g,) + (0,) * (arr.ndim - 1))
    act = pl.BlockSpec((seq, LANES), lambda g, b: (b, g))
    return pl.pallas_call(
        _s5_kernel, grid=(nblk, batch),
        in_specs=[act, blk(kc), blk(pc), blk(qc), blk(a)],
        out_specs=act, out_shape=jax.ShapeDtypeStruct((t, width), F32),
        scratch_shapes=[pltpu.VMEM((seq // S5_CHUNK, S5_CHUNK * LANES), F32),
                        pltpu.VMEM((LANES, S5_CHUNK * LANES), BF16),
                        pltpu.VMEM((S5_CHUNK, LANES, state_w), BF16),
                        pltpu.VMEM((state_w, S5_CHUNK * LANES), BF16)],
        compiler_params=_params("parallel", "arbitrary"), name="s5_scan",
    )(u, kc, pc, qc, a)


def _s5_matrices(lam_re, lam_im, log_step, b_re, b_im, c_re, c_im, chunks_per_seq):
    hi = lax.Precision.HIGHEST
    ln = S5_CHUNK
    gpb = LANES // S5_GROUP
    nblk = S5_GROUPS // gpb
    step = jnp.exp(log_step)[:, None]
    ar = lam_re * step
    ai = lam_im * step
    lbr = jnp.exp(ar) * jnp.cos(ai)
    lbi = jnp.exp(ar) * jnp.sin(ai)
    den = lam_re * lam_re + lam_im * lam_im
    cr = ((lbr - 1.0) * lam_re + lbi * lam_im) / den
    ci = (lbi * lam_re - (lbr - 1.0) * lam_im) / den
    bbr = cr[..., None] * b_re - ci[..., None] * b_im
    bbi = cr[..., None] * b_im + ci[..., None] * b_re
    dd = jnp.arange(ln + 1, dtype=F32)[:, None, None]
    pr = jnp.exp(dd * ar) * jnp.cos(dd * ai)
    pi = jnp.exp(dd * ar) * jnp.sin(dd * ai)
    tr = pr[:ln, :, :, None] * bbr - pi[:ln, :, :, None] * bbi
    ti = pr[:ln, :, :, None] * bbi + pi[:ln, :, :, None] * bbr
    kd = (jnp.einsum("gcp,dgpe->dgce", c_re, tr, precision=hi)
          - jnp.einsum("gcp,dgpe->dgce", c_im, ti, precision=hi))
    def compact(x):
        r, c = x.shape[-2:]
        return x.reshape(ln, nblk, gpb, r, c).transpose(1, 0, 4, 2, 3).reshape(nblk, ln, c, gpb * r)

    kc = compact(kd)
    pc = jnp.concatenate([compact(tr[::-1]), compact(ti[::-1])], axis=-1)
    q_re = c_re[None] * pr[1:, :, None, :] - c_im[None] * pi[1:, :, None, :]
    q_im = -(c_re[None] * pi[1:, :, None, :] + c_im[None] * pr[1:, :, None, :])
    qc = jnp.concatenate([compact(q_re), compact(q_im)], axis=2)
    levels = int(math.log2(chunks_per_seq))
    e = (ln * (2.0 ** jnp.arange(levels, dtype=F32)))[:, None, None]
    er = (jnp.exp(e * ar) * jnp.cos(e * ai)).reshape(levels, nblk, gpb * S5_STATE)
    ei = (jnp.exp(e * ar) * jnp.sin(e * ai)).reshape(levels, nblk, gpb * S5_STATE)
    a = jnp.stack([jnp.concatenate([er, er], -1), jnp.concatenate([-ei, ei], -1)], axis=1)
    a = a.transpose(2, 0, 1, 3).reshape(nblk, 2 * levels, 2 * gpb * S5_STATE)
    return kc, pc, qc, a


def _split_bf16(a):
    hi = a.astype(BF16)
    return hi, (a - hi.astype(F32)).astype(BF16)


def _mix_kernel(*refs, routed):
    (x_ref, attn_ref, ys_ref, u_ref, ga_ref, gb_ref, d_ref, wglu_ref, bglu_ref, wpa_ref, wps_ref,
     wout_ref, lng_ref, onesd_ref) = refs[:14]
    if routed:
        rw_ref, rb_ref, xo_ref, h_ref, route_ref, cnt_ref, cnt_scr = refs[14:]
    else:
        xo_ref, h_ref = refs[14:]
    y = ys_ref[...] + d_ref[...] * u_ref[...]
    y = jax.nn.gelu(y, approximate=True)
    ssm = y * jax.nn.sigmoid(_dot(y.astype(BF16), wglu_ref[...]) + bglu_ref[...])
    pa = _dot(attn_ref[...], wpa_ref[...])
    ps = _dot(ssm.astype(BF16), wps_ref[...])
    mix = ga_ref[...].astype(F32) * pa + gb_ref[...].astype(F32) * ps
    xn = x_ref[...] + _dot(mix.astype(BF16), wout_ref[...])
    xo_ref[...] = xn
    h2 = _rms(xn, lng_ref[...], onesd_ref)
    if not routed:
        h_ref[...] = h2.astype(BF16)
        return
    h_ref[...] = h2
    h_hi, h_lo = _split_bf16(h2)
    r_hi, r_lo = _split_bf16(rw_ref[...])
    logits = _dot(h_hi, r_hi) + _dot(h_lo, r_hi) + _dot(h_hi, r_lo) + rb_ref[...]
    lane = lax.broadcasted_iota(jnp.int32, logits.shape, 1).astype(F32)
    m1 = jnp.max(logits, axis=-1, keepdims=True)
    i1 = jnp.min(jnp.where(logits == m1, lane, float(ROUTER_PAD)), axis=-1, keepdims=True)
    rest = jnp.where(lane == i1, NEG_BIG, logits)
    m2 = jnp.max(rest, axis=-1, keepdims=True)
    i2 = jnp.min(jnp.where(rest == m2, lane, float(ROUTER_PAD)), axis=-1, keepdims=True)
    e2 = jnp.exp(m2 - m1)
    w1 = 1.0 / (1.0 + e2)
    @pl.when(pl.program_id(0) == 0)
    def _():
        cnt_scr[...] = jnp.zeros(cnt_scr.shape, F32)

    pick1 = lane == i1
    pick2 = lane == i2
    onehot = jnp.where(pick1, 1.0, 0.0) + jnp.where(pick2, 1.0, 0.0)
    tmr = onehot.shape[0]
    earlier = (lax.broadcasted_iota(jnp.int32, (tmr, tmr), 1)
               < lax.broadcasted_iota(jnp.int32, (tmr, tmr), 0))
    ranks = _dot(jnp.where(earlier, 1.0, 0.0).astype(BF16), onehot.astype(BF16)) + cnt_scr[0:1, :]
    r1 = jnp.sum(jnp.where(pick1, ranks, 0.0), axis=-1, keepdims=True)
    r2 = jnp.sum(jnp.where(pick2, ranks, 0.0), axis=-1, keepdims=True)
    cnt_scr[...] = cnt_scr[...] + jnp.sum(onehot, axis=0, keepdims=True)
    cnt_ref[...] = cnt_scr[...]
    route_ref[...] = jnp.where(lane == 0.0, i1, jnp.where(lane == 1.0, i2, jnp.where(
        lane == 2.0, w1, jnp.where(lane == 3.0, e2 * w1, jnp.where(
            lane == 4.0, r1, jnp.where(lane == 5.0, r2, 0.0))))))


def _mix(x2d, attn, ys, u, ga, gb, lp, tm, router=None):
    t, d = x2d.shape
    row = lambda i: (i, 0)
    routed = router is not None
    consts = [lp[k] for k in ("s5_d", "w_glu", "b_glu", "w_proj_attn", "w_proj_ssm", "w_out",
                              "ln_ffn_g", "ones_d")]
    if routed:
        consts += list(router)
    acts = [x2d, attn, ys, u, ga, gb]
    in_specs = [pl.BlockSpec((tm, a.shape[1]), row) for a in acts] + [_full(c.shape) for c in consts]
    if routed:
        out_shape = [jax.ShapeDtypeStruct((t, d), F32), jax.ShapeDtypeStruct((t, d), F32),
                     jax.ShapeDtypeStruct((t, ROUTER_PAD), F32)]
    else:
        out_shape = [jax.ShapeDtypeStruct((t, d), F32), jax.ShapeDtypeStruct((t, d), BF16)]
    out_specs = [pl.BlockSpec((tm, o.shape[1]), row) for o in out_shape]
    scratch = []
    if routed:
        out_shape.append(jax.ShapeDtypeStruct((SUBLANES, ROUTER_PAD), F32))
        out_specs.append(_full((SUBLANES, ROUTER_PAD)))
        scratch.append(pltpu.VMEM((SUBLANES, ROUTER_PAD), F32))
    return pl.pallas_call(
        functools.partial(_mix_kernel, routed=routed), grid=(t // tm,), in_specs=in_specs,
        out_specs=tuple(out_specs), out_shape=tuple(out_shape), scratch_shapes=scratch,
        compiler_params=_params("arbitrary" if routed else "parallel"),
        name="mix_routed" if routed else "mix",
    )(*acts, *consts)


def _ffn_kernel(te_ref, nu_ref, *refs, routed):
    if routed:
        x_ref, wg_ref, wu_ref, wd_ref, _, o_ref, acc_scr, x_scr = refs
    else:
        x_ref, wg_ref, wu_ref, wd_ref, res_ref, o_ref, acc_scr = refs
    i = pl.program_id(0)
    j = pl.program_id(1)
    last = pl.num_programs(1) - 1
    used = i < nu_ref[0]

    if routed:
        @pl.when(jnp.logical_and(used, j == 0))
        def _():
            x_scr[...] = x_ref[...].astype(BF16)

    @pl.when(used)
    def _():
        x = x_scr[...] if routed else x_ref[...]
        g = _dot(x, wg_ref[0].astype(BF16))
        a = (g * jax.nn.sigmoid(g) * _dot(x, wu_ref[0].astype(BF16))).astype(BF16)
        part = _dot(a, wd_ref[0].astype(BF16))

        @pl.when(j == 0)
        def _():
            acc_scr[...] = part

        @pl.when(j > 0)
        def _():
            acc_scr[...] += part

    @pl.when(jnp.logical_and(used, j == last))
    def _():
        if routed:
            o_ref[...] = acc_scr[...]
        else:
            o_ref[...] = acc_scr[...] + res_ref[...]

    @pl.when(jnp.logical_and(jnp.logical_not(used), j == last))
    def _():
        o_ref[...] = jnp.zeros(o_ref.shape, o_ref.dtype)


def _grouped_ffn(tile_expert, n_used, xs, wg, wu, wd, res, tm, tf, ybuf=None, tile_offset=0):
    routed = res is None
    n = xs.shape[0]
    d = wg.shape[1]
    nff = wg.shape[2] // tf

    def jeff(i, j, nu):
        return jnp.where(i < nu[0], j, nff - 1)

    row = lambda i, j, te, nu: (i, 0)
    in_specs = [
        pl.BlockSpec((tm, xs.shape[1]), row),
        pl.BlockSpec((1, d, tf), lambda i, j, te, nu: (te[i], 0, jeff(i, j, nu))),
        pl.BlockSpec((1, d, tf), lambda i, j, te, nu: (te[i], 0, jeff(i, j, nu))),
        pl.BlockSpec((1, tf, d), lambda i, j, te, nu: (te[i], jeff(i, j, nu), 0)),
    ]
    args = [xs, wg, wu, wd]
    scratch = [pltpu.VMEM((tm, d), F32)]
    aliases = {}
    if routed:
        scratch.append(pltpu.VMEM((tm, d), BF16))
        in_specs.append(pl.BlockSpec(memory_space=pl.ANY))
        args.append(ybuf)
        aliases = {2 + len(args) - 1: 0}
        out_rows = ybuf.shape[0]
    else:
        in_specs.append(pl.BlockSpec((tm, d), row))
        args.append(res)
        out_rows = n
    grid_spec = pltpu.PrefetchScalarGridSpec(
        num_scalar_prefetch=2, grid=(n // tm, nff), in_specs=in_specs,
        out_specs=pl.BlockSpec((tm, d), lambda i, j, te, nu: (i + tile_offset, 0)), scratch_shapes=scratch)
    return pl.pallas_call(
        functools.partial(_ffn_kernel, routed=routed), grid_spec=grid_spec,
        out_shape=jax.ShapeDtypeStruct((out_rows, d), F32), input_output_aliases=aliases,
        compiler_params=_params("parallel", "arbitrary"),
        name="ffn_experts" if routed else "ffn_dense",
    )(tile_expert, n_used, *args)


def _combine_kernel(x_ref, y0_ref, y1_ref, route_ref, o_ref):
    r = route_ref[...]
    w0 = r[:, 2:3]
    w1 = r[:, 3:4]
    o_ref[...] = x_ref[...] + w0 * y0_ref[...] + w1 * y1_ref[...]


def _combine(x2d, y0, y1, route, tm):
    t, d = x2d.shape
    row = lambda i: (i, 0)
    return pl.pallas_call(
        _combine_kernel, grid=(t // tm,),
        in_specs=[pl.BlockSpec((tm, d), row), pl.BlockSpec((tm, d), row),
                  pl.BlockSpec((tm, d), row), pl.BlockSpec((tm, ROUTER_PAD), row)],
        out_specs=pl.BlockSpec((tm, d), row), out_shape=jax.ShapeDtypeStruct((t, d), F32),
        compiler_params=_params("parallel"), name="combine",
    )(x2d, y0, y1, route)


def _rope_tables(seq):
    half = QK_ROPE_DIM // 2
    pos = jnp.arange(seq, dtype=F32)
    inv = jnp.power(ROPE_THETA, -jnp.arange(0, QK_ROPE_DIM, 2, dtype=F32) / QK_ROPE_DIM)
    ang = pos[:, None] * inv[None, :]
    cos, sin = jnp.cos(ang), jnp.sin(ang)
    tail = jnp.zeros((seq, HEAD_PAD - QK_HEAD_DIM), F32)
    c = jnp.concatenate([jnp.ones((seq, QK_NOPE_DIM), F32), cos, cos, tail], axis=1)
    s = jnp.concatenate([jnp.zeros((seq, QK_NOPE_DIM), F32), -sin, sin, tail], axis=1)
    return c, s


def _rope_partner(a):
    half = QK_ROPE_DIM // 2
    lead = a.shape[:-1]
    a = a.reshape(lead + (-1, HEAD_PAD))
    x1 = a[..., QK_NOPE_DIM:QK_NOPE_DIM + half]
    x2 = a[..., QK_NOPE_DIM + half:QK_HEAD_DIM]
    out = jnp.concatenate([jnp.zeros_like(a[..., :QK_NOPE_DIM]), x2, x1,
                           jnp.zeros_like(a[..., QK_HEAD_DIM:])], axis=-1)
    return out.reshape(lead + (-1,))


def _pad_heads(w, per_head, place_at=0):
    k = w.shape[0]
    w = w.reshape(k, N_HEADS, per_head)
    w = jnp.pad(w, ((0, 0), (0, 0), (place_at, HEAD_PAD - per_head - place_at)))
    return w.reshape(k, N_HEADS * HEAD_PAD)


def _layer_params(l, p, rope):
    d = p["w_in"].shape[1]
    w_in, b_in = p["w_in"][l], p["b_in"][l]
    o = [0, Q_LORA_RANK, Q_LORA_RANK + KV_LORA_RANK, Q_LORA_RANK + KV_LORA_RANK + QK_ROPE_DIM]
    o.append(o[3] + S5_WIDTH)
    o.append(o[4] + d)
    pad_rope = lambda a: jnp.pad(a, ((0, 0), (QK_NOPE_DIM, HEAD_PAD - QK_HEAD_DIM)))
    w_kr, b_kr = pad_rope(w_in[:, o[2]:o[3]]), pad_rope(b_in[None, o[2]:o[3]])
    w1 = jnp.concatenate([w_in[:, o[0]:o[2]], w_kr, _rope_partner(w_kr), w_in[:, o[3]:o[4]]], axis=1)
    b1 = jnp.concatenate([b_in[None, o[0]:o[2]], b_kr, _rope_partner(b_kr), b_in[None, o[3]:o[4]]], axis=1)
    w_ukv = p["w_ukv"][l].reshape(KV_LORA_RANK, N_HEADS, QK_NOPE_DIM + V_HEAD_DIM)
    wuq = _pad_heads(p["w_uq"][l], QK_HEAD_DIM)
    pad_g = lambda g: jnp.pad(g[None, :], ((0, 0), (0, HEAD_PAD - QK_HEAD_DIM)))
    qg, kg = pad_g(p["q_head_g"][l]), pad_g(p["k_head_g"][l])
    cos, sin = rope
    qscale = QK_HEAD_DIM ** -0.5 * LOG2E
    head_ones = jnp.kron(jnp.eye(MXU_DIM // HEAD_PAD, dtype=F32), jnp.ones((HEAD_PAD, HEAD_PAD), F32))
    return {
        "ln_mix_g": p["ln_mix_g"][l][None], "ones_d": jnp.ones((d, LANES), BF16),
        "w1": w1.astype(BF16), "b1": b1, "w2": w_in[:, o[4]:].astype(BF16), "b2": b_in[None, o[4]:],
        "q_norm_g": p["q_norm_g"][l][None], "ones_q": jnp.ones((Q_LORA_RANK, LANES), BF16),
        "kv_norm_g": p["kv_norm_g"][l][None], "ones_kv": jnp.ones((KV_LORA_RANK, LANES), BF16),
        "wuq": wuq.astype(BF16), "wuq_rot": _rope_partner(wuq).astype(BF16),
        "wk": _pad_heads(w_ukv[:, :, :QK_NOPE_DIM].reshape(KV_LORA_RANK, -1), QK_NOPE_DIM).astype(BF16),
        "wv": _pad_heads(w_ukv[:, :, QK_NOPE_DIM:].reshape(KV_LORA_RANK, -1), V_HEAD_DIM).astype(BF16),
        "vones": _pad_heads(jnp.ones((1, N_HEADS * (HEAD_PAD - V_HEAD_DIM)), F32),
                            HEAD_PAD - V_HEAD_DIM, place_at=V_HEAD_DIM),
        "head_ones": head_ones.astype(BF16),
        "qa": qg * cos * qscale, "qb": _rope_partner(qg) * sin * qscale,
        "ka": kg * cos, "kb": _rope_partner(kg) * sin,
        "s5_d": p["s5_d"][l][None], "w_glu": p["w_glu"][l].astype(BF16), "b_glu": p["b_glu"][l][None],
        "w_proj_attn": p["w_proj_attn"][l].astype(BF16), "w_proj_ssm": p["w_proj_ssm"][l].astype(BF16),
        "w_out": p["w_out"][l].astype(BF16), "ln_ffn_g": p["ln_ffn_g"][l][None],
    }


def _routing(route, counts, tm):
    t = route.shape[0]
    e = route[:, :TOP_K].astype(jnp.int32)
    rank = route[:, 2 * TOP_K:3 * TOP_K].astype(jnp.int32)
    cnt = counts[0, :N_EXPERTS].astype(jnp.int32)
    padded = ((cnt + tm - 1) // tm) * tm
    ends = jnp.cumsum(padded)
    starts = ends - padded
    pick = e[:, :, None] == jnp.arange(N_EXPERTS)[None, None, :]
    slots = rank + jnp.sum(jnp.where(pick, starts[None, None, :], 0), axis=-1)
    n_slots = TOP_K * t + N_EXPERTS * tm
    tok = jnp.broadcast_to(jnp.arange(t, dtype=jnp.int32)[:, None], (t, TOP_K))
    tok_of_slot = jnp.zeros((n_slots,), jnp.int32).at[slots.reshape(-1)].set(
        tok.reshape(-1), unique_indices=True)
    tile_start = jnp.arange(n_slots // tm, dtype=jnp.int32) * tm
    tile_expert = jnp.sum((tile_start[:, None] >= ends[None, :]).astype(jnp.int32), axis=1)
    tile_expert = jnp.minimum(tile_expert, N_EXPERTS - 1)
    n_used = (ends[-1] // tm).astype(jnp.int32)[None]
    tile_expert = jnp.where(tile_start < ends[-1], tile_expert, tile_expert[jnp.maximum(n_used[0] - 1, 0)])
    return tok_of_slot, tile_expert.astype(jnp.int32), n_used, slots


def kernel(x, ln_mix_g, w_in, b_in, q_norm_g, w_uq, kv_norm_g, w_ukv, q_head_g, k_head_g,
           s5_lam_re, s5_lam_im, s5_log_step, s5_b_re, s5_b_im, s5_c_re, s5_c_im, s5_d,
           w_glu, b_glu, w_proj_attn, w_proj_ssm, w_out, ln_ffn_g,
           ffn_w_gate, ffn_w_up, ffn_w_down, router_w, router_b,
           moe_w_gate, moe_w_up, moe_w_down):
    batch, seq, d = x.shape
    depth = w_in.shape[0]
    t = batch * seq
    tm = min(512, seq)
    tq = min(512, seq)
    tm_ffn = min(1024, t)
    chunks_per_seq = seq // S5_CHUNK
    assert chunks_per_seq & (chunks_per_seq - 1) == 0 and seq % tm == 0 and t % tm_ffn == 0
    p = dict(ln_mix_g=ln_mix_g, w_in=w_in, b_in=b_in, q_norm_g=q_norm_g, w_uq=w_uq,
             kv_norm_g=kv_norm_g, w_ukv=w_ukv, q_head_g=q_head_g, k_head_g=k_head_g, s5_d=s5_d,
             w_glu=w_glu, b_glu=b_glu, w_proj_attn=w_proj_attn, w_proj_ssm=w_proj_ssm,
             w_out=w_out, ln_ffn_g=ln_ffn_g)
    rope = _rope_tables(seq)
    x2d = x.reshape(t, d)
    for l in range(depth):
        lp = _layer_params(l, p, rope)
        q, k, v, u, ga, gb = _inproj(x2d, lp, seq, tm)
        attn = _attention(q, k, v, batch, seq, tq)
        mats = _s5_matrices(s5_lam_re[l], s5_lam_im[l], s5_log_step[l], s5_b_re[l], s5_b_im[l],
                            s5_c_re[l], s5_c_im[l], chunks_per_seq)
        ys = _s5_scan(u, mats, batch, seq)
        j = l // 2
        if l % 2 == 0:
            x2d, h2 = _mix(x2d, attn, ys, u, ga, gb, lp, tm)
            n_tiles = t // tm_ffn
            x2d = _grouped_ffn(
                jnp.zeros((n_tiles,), jnp.int32), jnp.full((1,), n_tiles, jnp.int32), h2,
                ffn_w_gate[j][None], ffn_w_up[j][None], ffn_w_down[j][None], x2d, tm_ffn, 256)
        else:
            rw = jnp.pad(router_w[j], ((0, 0), (0, ROUTER_PAD - N_EXPERTS)))
            rb = jnp.pad(router_b[j][None], ((0, 0), (0, ROUTER_PAD - N_EXPERTS)), constant_values=NEG_BIG)
            x2d, h2, route, counts = _mix(x2d, attn, ys, u, ga, gb, lp, tm, router=(rw, rb))
            tok_of_slot, tile_expert, n_used, slots = _routing(route, counts, tm_ffn)
            n_tiles = tok_of_slot.shape[0] // tm_ffn
            n_calls = EXPERT_CALLS if n_tiles % EXPERT_CALLS == 0 else 1
            per_call = n_tiles // n_calls
            ysort = jnp.zeros((tok_of_slot.shape[0], d), F32)
            for c in range(n_calls):
                lo = c * per_call
                xs = jnp.take(h2, tok_of_slot[lo * tm_ffn:(lo + per_call) * tm_ffn], axis=0)
                ysort = _grouped_ffn(tile_expert[lo:lo + per_call], n_used - lo, xs, moe_w_gate[j],
                                     moe_w_up[j], moe_w_down[j], None, tm_ffn, 512, ybuf=ysort, tile_offset=lo)
            x2d = _combine(x2d, jnp.take(ysort, slots[:, 0], axis=0),
                           jnp.take(ysort, slots[:, 1], axis=0), route, tm)
    return x2d.reshape(batch, seq, d)
```

```python
import functools
import math

import jax
import jax.numpy as jnp
from jax import lax
from jax.experimental import pallas as pl
from jax.experimental.pallas import tpu as pltpu

EPS = 1e-6
N_HEADS = 8
QK_NOPE_DIM = 64
QK_ROPE_DIM = 32
QK_HEAD_DIM = QK_NOPE_DIM + QK_ROPE_DIM
V_HEAD_DIM = 64
LANES = 128
SUBLANES = 8
HEAD_PAD = LANES
MXU_DIM = 256
Q_LORA_RANK = 256
KV_LORA_RANK = 128
ROPE_THETA = 10000.0
S5_WIDTH = 512
S5_GROUP = 16
S5_GROUPS = S5_WIDTH // S5_GROUP
S5_STATE = 64
S5_CHUNK = 16
N_EXPERTS = 8
TOP_K = 2
ROUTER_PAD = LANES
EXPERT_CALLS = 4
NEG_BIG = -1e30
LOG2E = 1.4426950408889634

VMEM_LIMIT_BYTES = 56 * 1024 * 1024

F32 = jnp.float32
BF16 = jnp.bfloat16


def _dot(a, b):
    return jnp.dot(a, b, preferred_element_type=F32)


def _lane_tile(a, n):
    return jnp.concatenate([a] * n, axis=1)


def _rms_scale(x, ones_ref):
    ssq = _dot((x * x).astype(BF16), ones_ref[...])
    return lax.rsqrt(ssq * (1.0 / x.shape[1]) + EPS)


def _rms(x, g, ones_ref):
    return x * _lane_tile(_rms_scale(x, ones_ref), x.shape[1] // LANES) * g


def _params(*semantics):
    return pltpu.CompilerParams(dimension_semantics=semantics, vmem_limit_bytes=VMEM_LIMIT_BYTES)


def _full(shape):
    return pl.BlockSpec(shape, lambda *_: (0,) * len(shape))


def _inproj_kernel(x_ref, g_ref, onesd_ref, w1_ref, b1_ref, w2_ref, b2_ref, qng_ref, onesq_ref,
                   kvng_ref, oneskv_ref, wuq_ref, wuqr_ref, wk_ref, wv_ref, vones_ref, hones_ref,
                   qa_ref, qb_ref, ka_ref, kb_ref,
                   q_ref, k_ref, v_ref, u_ref, ga_ref, gb_ref):
    d = x_ref.shape[1]
    h = _rms(x_ref[...], g_ref[...], onesd_ref).astype(BF16)
    p1 = _dot(h, w1_ref[...]) + b1_ref[...]
    p2 = _dot(h, w2_ref[...]) + b2_ref[...]
    ga_ref[...] = jax.nn.sigmoid(p2[:, :d]).astype(BF16)
    gb_ref[...] = jax.nn.sigmoid(p2[:, d:]).astype(BF16)
    c0 = Q_LORA_RANK
    c1 = c0 + KV_LORA_RANK
    c2 = c1 + HEAD_PAD
    c3 = c2 + HEAD_PAD
    u_ref[...] = p1[:, c3:]
    qn = _rms(p1[:, :c0], qng_ref[...], onesq_ref).astype(BF16)
    kvn = _rms(p1[:, c0:c1], kvng_ref[...], oneskv_ref).astype(BF16)
    v_ref[...] = (_dot(kvn, wv_ref[...]) + vones_ref[...]).astype(BF16)

    def head_norm_rope(raw, rot, a_tab, b_tab):
        chunks = []
        for c in range(raw.shape[1] // MXU_DIM):
            blk = raw[:, c * MXU_DIM:(c + 1) * MXU_DIM]
            chunks.append(_dot((blk * blk).astype(BF16), hones_ref[...]))
        rs = lax.rsqrt(jnp.concatenate(chunks, axis=1) * (1.0 / QK_HEAD_DIM) + EPS)
        return rs * (raw * _lane_tile(a_tab, N_HEADS) + rot * _lane_tile(b_tab, N_HEADS))

    q_raw = _dot(qn, wuq_ref[...])
    q_rot = _dot(qn, wuqr_ref[...])
    q_ref[...] = head_norm_rope(q_raw, q_rot, qa_ref[...], qb_ref[...]).astype(BF16)
    k_raw = _dot(kvn, wk_ref[...]) + _lane_tile(p1[:, c1:c2], N_HEADS)
    k_rot = _lane_tile(p1[:, c2:c3], N_HEADS)
    k_ref[...] = head_norm_rope(k_raw, k_rot, ka_ref[...], kb_ref[...]).astype(BF16)


def _inproj(x2d, lp, seq, tm):
    t, d = x2d.shape
    nseq = seq // tm
    hp = N_HEADS * HEAD_PAD
    row = lambda i: (i, 0)
    pos = lambda i: (i % nseq, 0)
    consts = [lp[k] for k in ("ln_mix_g", "ones_d", "w1", "b1", "w2", "b2", "q_norm_g", "ones_q",
                              "kv_norm_g", "ones_kv", "wuq", "wuq_rot", "wk", "wv", "vones", "head_ones")]
    tabs = [lp[k] for k in ("qa", "qb", "ka", "kb")]
    in_specs = ([pl.BlockSpec((tm, d), row)] + [_full(c.shape) for c in consts]
                + [pl.BlockSpec((tm, HEAD_PAD), pos)] * 4)
    wide = lambda n, dt: (jax.ShapeDtypeStruct((t, n), dt), pl.BlockSpec((tm, n), row))
    outs = [wide(hp, BF16), wide(hp, BF16), wide(hp, BF16), wide(S5_WIDTH, F32), wide(d, BF16), wide(d, BF16)]
    return pl.pallas_call(
        _inproj_kernel, grid=(t // tm,), in_specs=in_specs,
        out_specs=tuple(o[1] for o in outs), out_shape=tuple(o[0] for o in outs),
        compiler_params=_params("parallel"), name="inproj",
    )(x2d, *consts, *tabs)


def _attn_kernel(q_ref, k_ref, v_ref, o_ref, m_scr, acc_scr):
    i = pl.program_id(1)
    j = pl.program_id(2)
    tq = q_ref.shape[0]
    tk = k_ref.shape[0]

    @pl.when(j == 0)
    def _():
        m_scr[...] = jnp.full(m_scr.shape, -jnp.inf, F32)
        acc_scr[...] = jnp.zeros(acc_scr.shape, F32)

    def step(masked):
        if masked:
            rows = lax.broadcasted_iota(jnp.int32, (tq, tk), 0)
            cols = lax.broadcasted_iota(jnp.int32, (tq, tk), 1)
            keep = cols <= rows
        for hd in range(N_HEADS):
            sl = slice(hd * HEAD_PAD, (hd + 1) * HEAD_PAD)
            s = lax.dot_general(q_ref[:, sl], k_ref[:, sl], (((1,), (1,)), ((), ())),
                                preferred_element_type=F32)
            if masked:
                s = jnp.where(keep, s, -jnp.inf)
            parts = [s[:, c * LANES:(c + 1) * LANES] for c in range(tk // LANES)]
            blk_max = functools.reduce(jnp.maximum, parts)
            m_prev = m_scr[hd]
            m_new = jnp.maximum(m_prev, jnp.max(blk_max, axis=-1, keepdims=True))
            alpha = jnp.exp2(m_prev - m_new)
            p = jnp.concatenate([jnp.exp2(part - m_new) for part in parts], axis=1).astype(BF16)
            acc_scr[hd] = alpha * acc_scr[hd] + _dot(p, v_ref[:, sl])
            m_scr[hd] = m_new

    @pl.when(j < i)
    def _():
        step(False)

    @pl.when(j == i)
    def _():
        step(True)
        lane = lax.broadcasted_iota(jnp.int32, (tq, HEAD_PAD), 1)
        for pr in range(N_HEADS // 2):
            a0 = acc_scr[2 * pr]
            a1 = acc_scr[2 * pr + 1]
            lo = a0 / pltpu.roll(a0, V_HEAD_DIM, 1)
            hi = pltpu.roll(a1, V_HEAD_DIM, 1) / a1
            o_ref[:, pr * HEAD_PAD:(pr + 1) * HEAD_PAD] = jnp.where(lane < V_HEAD_DIM, lo, hi).astype(BF16)


def _attention(q, k, v, batch, seq, tq):
    nq = seq // tq
    hp = N_HEADS * HEAD_PAD
    vw = N_HEADS * V_HEAD_DIM
    qmap = lambda b, i, j: (b * nq + i, 0)
    kmap = lambda b, i, j: (b * nq + jnp.minimum(j, i), 0)
    return pl.pallas_call(
        _attn_kernel, grid=(batch, nq, nq),
        in_specs=[pl.BlockSpec((tq, hp), qmap), pl.BlockSpec((tq, hp), kmap),
                  pl.BlockSpec((tq, hp), kmap)],
        out_specs=pl.BlockSpec((tq, vw), qmap),
        out_shape=jax.ShapeDtypeStruct((batch * seq, vw), BF16),
        scratch_shapes=[pltpu.VMEM((N_HEADS, tq, HEAD_PAD), F32), pltpu.VMEM((N_HEADS, tq, HEAD_PAD), F32)],
        compiler_params=_params("parallel", "parallel", "arbitrary"), name="attention",
    )(q, k, v)


def _s5_kernel(u_ref, kc_ref, pc_ref, qc_ref, a_ref, y_ref, acc_scr, tcat_scr, p_scr, q_scr):
    ln = S5_CHUNK
    chunks = acc_scr.shape[0]
    half = a_ref.shape[2] // 2
    gpb = LANES // S5_GROUP

    @pl.when(pl.program_id(1) == 0)
    def _():
        def diag_mask(shape, row_div, col_div, col_mod):
            r = lax.broadcasted_iota(jnp.int32, shape, 0) // row_div
            c = (lax.broadcasted_iota(jnp.int32, shape, 1) % col_mod) // col_div
            return r == c

        rep = lambda x: jnp.concatenate([x] * gpb, axis=0)
        mask_t = diag_mask((LANES, LANES), S5_GROUP, S5_GROUP, LANES)
        mask_p = diag_mask((LANES, 2 * half), S5_GROUP, S5_STATE, half)
        mask_q = diag_mask((half, LANES), S5_STATE, S5_GROUP, LANES)
        for d in range(ln):
            tcat_scr[:, d * LANES:(d + 1) * LANES] = jnp.where(mask_t, rep(kc_ref[0, d]), 0.0).astype(BF16)
            p_scr[d] = jnp.where(mask_p, rep(pc_ref[0, d]), 0.0).astype(BF16)
            qc = qc_ref[0, d]
            q_scr[:half, d * LANES:(d + 1) * LANES] = jnp.where(mask_q, rep(qc[:S5_STATE]), 0.0).astype(BF16)
            q_scr[half:, d * LANES:(d + 1) * LANES] = jnp.where(mask_q, rep(qc[S5_STATE:]), 0.0).astype(BF16)

    z = jnp.zeros((chunks, 2 * half), F32)
    for s in range(ln):
        us = u_ref[pl.ds(s, chunks, stride=ln), :].astype(BF16)
        contrib = _dot(us, tcat_scr[:, :(ln - s) * LANES])
        if s == 0:
            acc_scr[...] = contrib
        else:
            acc_scr[:, s * LANES:] += contrib
        z = z + _dot(us, p_scr[s])
    row = lax.broadcasted_iota(jnp.int32, z.shape, 0)
    a = a_ref[0]
    off = 1
    lvl = 0
    while off < chunks:
        prev = pltpu.roll(z, off, 0)
        contrib = prev * a[2 * lvl:2 * lvl + 1] + pltpu.roll(prev, half, 1) * a[2 * lvl + 1:2 * lvl + 2]
        z = z + jnp.where(row >= off, contrib, 0.0)
        off *= 2
        lvl += 1
    x_start = jnp.where(row >= 1, pltpu.roll(z, 1, 0), 0.0)
    y = acc_scr[...] + _dot(x_start.astype(BF16), q_scr[...])
    for t in range(ln):
        y_ref[pl.ds(t, chunks, stride=ln), :] = y[:, t * LANES:(t + 1) * LANES]


def _s5_scan(u, mats, batch, seq):
    t, width = u.shape
    kc, pc, qc, a = mats
    nblk = width // LANES
    state_w = a.shape[2]
    blk = lambda arr: pl.BlockSpec((1,) + arr.shape[1:], lambda g, b: (g,) + (0,) * (arr.ndim - 1))
    act = pl.BlockSpec((seq, LANES), lambda g, b: (b, g))
    return pl.pallas_call(
        _s5_kernel, grid=(nblk, batch),
        in_specs=[act, blk(kc), blk(pc), blk(qc), blk(a)],
        out_specs=act, out_shape=jax.ShapeDtypeStruct((t, width), F32),
        scratch_shapes=[pltpu.VMEM((seq // S5_CHUNK, S5_CHUNK * LANES), F32),
                        pltpu.VMEM((LANES, S5_CHUNK * LANES), BF16),
                        pltpu.VMEM((S5_CHUNK, LANES, state_w), BF16),
                        pltpu.VMEM((state_w, S5_CHUNK * LANES), BF16)],
        compiler_params=_params("parallel", "arbitrary"), name="s5_scan",
    )(u, kc, pc, qc, a)


def _s5_matrices(lam_re, lam_im, log_step, b_re, b_im, c_re, c_im, chunks_per_seq):
    hi = lax.Precision.HIGHEST
    ln = S5_CHUNK
    gpb = LANES // S5_GROUP
    nblk = S5_GROUPS // gpb
    step = jnp.exp(log_step)[:, None]
    ar = lam_re * step
    ai = lam_im * step
    lbr = jnp.exp(ar) * jnp.cos(ai)
    lbi = jnp.exp(ar) * jnp.sin(ai)
    den = lam_re * lam_re + lam_im * lam_im
    cr = ((lbr - 1.0) * lam_re + lbi * lam_im) / den
    ci = (lbi * lam_re - (lbr - 1.0) * lam_im) / den
    bbr = cr[..., None] * b_re - ci[..., None] * b_im
    bbi = cr[..., None] * b_im + ci[..., None] * b_re
    dd = jnp.arange(ln + 1, dtype=F32)[:, None, None]
    pr = jnp.exp(dd * ar) * jnp.cos(dd * ai)
    pi = jnp.exp(dd * ar) * jnp.sin(dd * ai)
    tr = pr[:ln, :, :, None] * bbr - pi[:ln, :, :, None] * bbi
    ti = pr[:ln, :, :, None] * bbi + pi[:ln, :, :, None] * bbr
    kd = (jnp.einsum("gcp,dgpe->dgce", c_re, tr, precision=hi)
          - jnp.einsum("gcp,dgpe->dgce", c_im, ti, precision=hi))
    def compact(x):
        r, c = x.shape[-2:]
        return x.reshape(ln, nblk, gpb, r, c).transpose(1, 0, 4, 2, 3).reshape(nblk, ln, c, gpb * r)

    kc = compact(kd)
    pc = jnp.concatenate([compact(tr[::-1]), compact(ti[::-1])], axis=-1)
    q_re = c_re[None] * pr[1:, :, None, :] - c_im[None] * pi[1:, :, None, :]
    q_im = -(c_re[None] * pi[1:, :, None, :] + c_im[None] * pr[1:, :, None, :])
    qc = jnp.concatenate([compact(q_re), compact(q_im)], axis=2)
    levels = int(math.log2(chunks_per_seq))
    e = (ln * (2.0 ** jnp.arange(levels, dtype=F32)))[:, None, None]
    er = (jnp.exp(e * ar) * jnp.cos(e * ai)).reshape(levels, nblk, gpb * S5_STATE)
    ei = (jnp.exp(e * ar) * jnp.sin(e * ai)).reshape(levels, nblk, gpb * S5_STATE)
    a = jnp.stack([jnp.concatenate([er, er], -1), jnp.concatenate([-ei, ei], -1)], axis=1)
    a = a.transpose(2, 0, 1, 3).reshape(nblk, 2 * levels, 2 * gpb * S5_STATE)
    return kc, pc, qc, a


def _split_bf16(a):
    hi = a.astype(BF16)
    return hi, (a - hi.astype(F32)).astype(BF16)


def _mix_kernel(*refs, routed):
    (x_ref, attn_ref, ys_ref, u_ref, ga_ref, gb_ref, d_ref, wglu_ref, bglu_ref, wpa_ref, wps_ref,
     wout_ref, lng_ref, onesd_ref) = refs[:14]
    if routed:
        rw_ref, rb_ref, xo_ref, h_ref, route_ref, cnt_ref, cnt_scr = refs[14:]
    else:
        xo_ref, h_ref = refs[14:]
    y = ys_ref[...] + d_ref[...] * u_ref[...]
    y = jax.nn.gelu(y, approximate=True)
    ssm = y * jax.nn.sigmoid(_dot(y.astype(BF16), wglu_ref[...]) + bglu_ref[...])
    pa = _dot(attn_ref[...], wpa_ref[...])
    ps = _dot(ssm.astype(BF16), wps_ref[...])
    mix = ga_ref[...].astype(F32) * pa + gb_ref[...].astype(F32) * ps
    xn = x_ref[...] + _dot(mix.astype(BF16), wout_ref[...])
    xo_ref[...] = xn
    h2 = _rms(xn, lng_ref[...], onesd_ref)
    if not routed:
        h_ref[...] = h2.astype(BF16)
        return
    h_ref[...] = h2
    h_hi, h_lo = _split_bf16(h2)
    r_hi, r_lo = _split_bf16(rw_ref[...])
    logits = _dot(h_hi, r_hi) + _dot(h_lo, r_hi) + _dot(h_hi, r_lo) + rb_ref[...]
    lane = lax.broadcasted_iota(jnp.int32, logits.shape, 1).astype(F32)
    m1 = jnp.max(logits, axis=-1, keepdims=True)
    i1 = jnp.min(jnp.where(logits == m1, lane, float(ROUTER_PAD)), axis=-1, keepdims=True)
    rest = jnp.where(lane == i1, NEG_BIG, logits)
    m2 = jnp.max(rest, axis=-1, keepdims=True)
    i2 = jnp.min(jnp.where(rest == m2, lane, float(ROUTER_PAD)), axis=-1, keepdims=True)
    e2 = jnp.exp(m2 - m1)
    w1 = 1.0 / (1.0 + e2)
    @pl.when(pl.program_id(0) == 0)
    def _():
        cnt_scr[...] = jnp.zeros(cnt_scr.shape, F32)

    pick1 = lane == i1
    pick2 = lane == i2
    onehot = jnp.where(pick1, 1.0, 0.0) + jnp.where(pick2, 1.0, 0.0)
    tmr = onehot.shape[0]
    earlier = (lax.broadcasted_iota(jnp.int32, (tmr, tmr), 1)
               < lax.broadcasted_iota(jnp.int32, (tmr, tmr), 0))
    ranks = _dot(jnp.where(earlier, 1.0, 0.0).astype(BF16), onehot.astype(BF16)) + cnt_scr[0:1, :]
    r1 = jnp.sum(jnp.where(pick1, ranks, 0.0), axis=-1, keepdims=True)
    r2 = jnp.sum(jnp.where(pick2, ranks, 0.0), axis=-1, keepdims=True)
    cnt_scr[...] = cnt_scr[...] + jnp.sum(onehot, axis=0, keepdims=True)
    cnt_ref[...] = cnt_scr[...]
    route_ref[...] = jnp.where(lane == 0.0, i1, jnp.where(lane == 1.0, i2, jnp.where(
        lane == 2.0, w1, jnp.where(lane == 3.0, e2 * w1, jnp.where(
            lane == 4.0, r1, jnp.where(lane == 5.0, r2, 0.0))))))


def _mix(x2d, attn, ys, u, ga, gb, lp, tm, router=None):
    t, d = x2d.shape
    row = lambda i: (i, 0)
    routed = router is not None
    consts = [lp[k] for k in ("s5_d", "w_glu", "b_glu", "w_proj_attn", "w_proj_ssm", "w_out",
                              "ln_ffn_g", "ones_d")]
    if routed:
        consts += list(router)
    acts = [x2d, attn, ys, u, ga, gb]
    in_specs = [pl.BlockSpec((tm, a.shape[1]), row) for a in acts] + [_full(c.shape) for c in consts]
    if routed:
        out_shape = [jax.ShapeDtypeStruct((t, d), F32), jax.ShapeDtypeStruct((t, d), F32),
                     jax.ShapeDtypeStruct((t, ROUTER_PAD), F32)]
    else:
        out_shape = [jax.ShapeDtypeStruct((t, d), F32), jax.ShapeDtypeStruct((t, d), BF16)]
    out_specs = [pl.BlockSpec((tm, o.shape[1]), row) for o in out_shape]
    scratch = []
    if routed:
        out_shape.append(jax.ShapeDtypeStruct((SUBLANES, ROUTER_PAD), F32))
        out_specs.append(_full((SUBLANES, ROUTER_PAD)))
        scratch.append(pltpu.VMEM((SUBLANES, ROUTER_PAD), F32))
    return pl.pallas_call(
        functools.partial(_mix_kernel, routed=routed), grid=(t // tm,), in_specs=in_specs,
        out_specs=tuple(out_specs), out_shape=tuple(out_shape), scratch_shapes=scratch,
        compiler_params=_params("arbitrary" if routed else "parallel"),
        name="mix_routed" if routed else "mix",
    )(*acts, *consts)


def _ffn_kernel(te_ref, nu_ref, *refs, routed):
    if routed:
        x_ref, wg_ref, wu_ref, wd_ref, _, o_ref, acc_scr, x_scr = refs
    else:
        x_ref, wg_ref, wu_ref, wd_ref, res_ref, o_ref, acc_scr = refs
    i = pl.program_id(0)
    j = pl.program_id(1)
    last = pl.num_programs(1) - 1
    used = i < nu_ref[0]

    if routed:
        @pl.when(jnp.logical_and(used, j == 0))
        def _():
            x_scr[...] = x_ref[...].astype(BF16)

    @pl.when(used)
    def _():
        x = x_scr[...] if routed else x_ref[...]
        g = _dot(x, wg_ref[0].astype(BF16))
        a = (g * jax.nn.sigmoid(g) * _dot(x, wu_ref[0].astype(BF16))).astype(BF16)
        part = _dot(a, wd_ref[0].astype(BF16))

        @pl.when(j == 0)
        def _():
            acc_scr[...] = part

        @pl.when(j > 0)
        def _():
            acc_scr[...] += part

    @pl.when(jnp.logical_and(used, j == last))
    def _():
        if routed:
            o_ref[...] = acc_scr[...]
        else:
            o_ref[...] = acc_scr[...] + res_ref[...]

    @pl.when(jnp.logical_and(jnp.logical_not(used), j == last))
    def _():
        o_ref[...] = jnp.zeros(o_ref.shape, o_ref.dtype)


def _grouped_ffn(tile_expert, n_used, xs, wg, wu, wd, res, tm, tf, ybuf=None, tile_offset=0):
    routed = res is None
    n = xs.shape[0]
    d = wg.shape[1]
    nff = wg.shape[2] // tf

    def jeff(i, j, nu):
        return jnp.where(i < nu[0], j, nff - 1)

    row = lambda i, j, te, nu: (i, 0)
    in_specs = [
        pl.BlockSpec((tm, xs.shape[1]), row),
        pl.BlockSpec((1, d, tf), lambda i, j, te, nu: (te[i], 0, jeff(i, j, nu))),
        pl.BlockSpec((1, d, tf), lambda i, j, te, nu: (te[i], 0, jeff(i, j, nu))),
        pl.BlockSpec((1, tf, d), lambda i, j, te, nu: (te[i], jeff(i, j, nu), 0)),
    ]
    args = [xs, wg, wu, wd]
    scratch = [pltpu.VMEM((tm, d), F32)]
    aliases = {}
    if routed:
        scratch.append(pltpu.VMEM((tm, d), BF16))
        in_specs.append(pl.BlockSpec(memory_space=pl.ANY))
        args.append(ybuf)
        aliases = {2 + len(args) - 1: 0}
        out_rows = ybuf.shape[0]
    else:
        in_specs.append(pl.BlockSpec((tm, d), row))
        args.append(res)
        out_rows = n
    grid_spec = pltpu.PrefetchScalarGridSpec(
        num_scalar_prefetch=2, grid=(n // tm, nff), in_specs=in_specs,
        out_specs=pl.BlockSpec((tm, d), lambda i, j, te, nu: (i + tile_offset, 0)), scratch_shapes=scratch)
    return pl.pallas_call(
        functools.partial(_ffn_kernel, routed=routed), grid_spec=grid_spec,
        out_shape=jax.ShapeDtypeStruct((out_rows, d), F32), input_output_aliases=aliases,
        compiler_params=_params("parallel", "arbitrary"),
        name="ffn_experts" if routed else "ffn_dense",
    )(tile_expert, n_used, *args)


def _combine_kernel(x_ref, y0_ref, y1_ref, route_ref, o_ref):
    r = route_ref[...]
    w0 = r[:, 2:3]
    w1 = r[:, 3:4]
    o_ref[...] = x_ref[...] + w0 * y0_ref[...] + w1 * y1_ref[...]


def _combine(x2d, y0, y1, route, tm):
    t, d = x2d.shape
    row = lambda i: (i, 0)
    return pl.pallas_call(
        _combine_kernel, grid=(t // tm,),
        in_specs=[pl.BlockSpec((tm, d), row), pl.BlockSpec((tm, d), row),
                  pl.BlockSpec((tm, d), row), pl.BlockSpec((tm, ROUTER_PAD), row)],
        out_specs=pl.BlockSpec((tm, d), row), out_shape=jax.ShapeDtypeStruct((t, d), F32),
        compiler_params=_params("parallel"), name="combine",
    )(x2d, y0, y1, route)


def _rope_tables(seq):
    half = QK_ROPE_DIM // 2
    pos = jnp.arange(seq, dtype=F32)
    inv = jnp.power(ROPE_THETA, -jnp.arange(0, QK_ROPE_DIM, 2, dtype=F32) / QK_ROPE_DIM)
    ang = pos[:, None] * inv[None, :]
    cos, sin = jnp.cos(ang), jnp.sin(ang)
    tail = jnp.zeros((seq, HEAD_PAD - QK_HEAD_DIM), F32)
    c = jnp.concatenate([jnp.ones((seq, QK_NOPE_DIM), F32), cos, cos, tail], axis=1)
    s = jnp.concatenate([jnp.zeros((seq, QK_NOPE_DIM), F32), -sin, sin, tail], axis=1)
    return c, s


def _rope_partner(a):
    half = QK_ROPE_DIM // 2
    lead = a.shape[:-1]
    a = a.reshape(lead + (-1, HEAD_PAD))
    x1 = a[..., QK_NOPE_DIM:QK_NOPE_DIM + half]
    x2 = a[..., QK_NOPE_DIM + half:QK_HEAD_DIM]
    out = jnp.concatenate([jnp.zeros_like(a[..., :QK_NOPE_DIM]), x2, x1,
                           jnp.zeros_like(a[..., QK_HEAD_DIM:])], axis=-1)
    return out.reshape(lead + (-1,))


def _pad_heads(w, per_head, place_at=0):
    k = w.shape[0]
    w = w.reshape(k, N_HEADS, per_head)
    w = jnp.pad(w, ((0, 0), (0, 0), (place_at, HEAD_PAD - per_head - place_at)))
    return w.reshape(k, N_HEADS * HEAD_PAD)


def _layer_params(l, p, rope):
    d = p["w_in"].shape[1]
    w_in, b_in = p["w_in"][l], p["b_in"][l]
    o = [0, Q_LORA_RANK, Q_LORA_RANK + KV_LORA_RANK, Q_LORA_RANK + KV_LORA_RANK + QK_ROPE_DIM]
    o.append(o[3] + S5_WIDTH)
    o.append(o[4] + d)
    pad_rope = lambda a: jnp.pad(a, ((0, 0), (QK_NOPE_DIM, HEAD_PAD - QK_HEAD_DIM)))
    w_kr, b_kr = pad_rope(w_in[:, o[2]:o[3]]), pad_rope(b_in[None, o[2]:o[3]])
    w1 = jnp.concatenate([w_in[:, o[0]:o[2]], w_kr, _rope_partner(w_kr), w_in[:, o[3]:o[4]]], axis=1)
    b1 = jnp.concatenate([b_in[None, o[0]:o[2]], b_kr, _rope_partner(b_kr), b_in[None, o[3]:o[4]]], axis=1)
    w_ukv = p["w_ukv"][l].reshape(KV_LORA_RANK, N_HEADS, QK_NOPE_DIM + V_HEAD_DIM)
    wuq = _pad_heads(p["w_uq"][l], QK_HEAD_DIM)
    pad_g = lambda g: jnp.pad(g[None, :], ((0, 0), (0, HEAD_PAD - QK_HEAD_DIM)))
    qg, kg = pad_g(p["q_head_g"][l]), pad_g(p["k_head_g"][l])
    cos, sin = rope
    qscale = QK_HEAD_DIM ** -0.5 * LOG2E
    head_ones = jnp.kron(jnp.eye(MXU_DIM // HEAD_PAD, dtype=F32), jnp.ones((HEAD_PAD, HEAD_PAD), F32))
    return {
        "ln_mix_g": p["ln_mix_g"][l][None], "ones_d": jnp.ones((d, LANES), BF16),
        "w1": w1.astype(BF16), "b1": b1, "w2": w_in[:, o[4]:].astype(BF16), "b2": b_in[None, o[4]:],
        "q_norm_g": p["q_norm_g"][l][None], "ones_q": jnp.ones((Q_LORA_RANK, LANES), BF16),
        "kv_norm_g": p["kv_norm_g"][l][None], "ones_kv": jnp.ones((KV_LORA_RANK, LANES), BF16),
        "wuq": wuq.astype(BF16), "wuq_rot": _rope_partner(wuq).astype(BF16),
        "wk": _pad_heads(w_ukv[:, :, :QK_NOPE_DIM].reshape(KV_LORA_RANK, -1), QK_NOPE_DIM).astype(BF16),
        "wv": _pad_heads(w_ukv[:, :, QK_NOPE_DIM:].reshape(KV_LORA_RANK, -1), V_HEAD_DIM).astype(BF16),
        "vones": _pad_heads(jnp.ones((1, N_HEADS * (HEAD_PAD - V_HEAD_DIM)), F32),
                            HEAD_PAD - V_HEAD_DIM, place_at=V_HEAD_DIM),
        "head_ones": head_ones.astype(BF16),
        "qa": qg * cos * qscale, "qb": _rope_partner(qg) * sin * qscale,
        "ka": kg * cos, "kb": _rope_partner(kg) * sin,
        "s5_d": p["s5_d"][l][None], "w_glu": p["w_glu"][l].astype(BF16), "b_glu": p["b_glu"][l][None],
        "w_proj_attn": p["w_proj_attn"][l].astype(BF16), "w_proj_ssm": p["w_proj_ssm"][l].astype(BF16),
        "w_out": p["w_out"][l].astype(BF16), "ln_ffn_g": p["ln_ffn_g"][l][None],
    }


def _routing(route, counts, tm):
    t = route.shape[0]
    e = route[:, :TOP_K].astype(jnp.int32)
    rank = route[:, 2 * TOP_K:3 * TOP_K].astype(jnp.int32)
    cnt = counts[0, :N_EXPERTS].astype(jnp.int32)
    padded = ((cnt + tm - 1) // tm) * tm
    ends = jnp.cumsum(padded)
    starts = ends - padded
    pick = e[:, :, None] == jnp.arange(N_EXPERTS)[None, None, :]
    slots = rank + jnp.sum(jnp.where(pick, starts[None, None, :], 0), axis=-1)
    n_slots = TOP_K * t + N_EXPERTS * tm
    tok = jnp.broadcast_to(jnp.arange(t, dtype=jnp.int32)[:, None], (t, TOP_K))
    tok_of_slot = jnp.zeros((n_slots,), jnp.int32).at[slots.reshape(-1)].set(
        tok.reshape(-1), unique_indices=True)
    tile_start = jnp.arange(n_slots // tm, dtype=jnp.int32) * tm
    tile_expert = jnp.sum((tile_start[:, None] >= ends[None, :]).astype(jnp.int32), axis=1)
    tile_expert = jnp.minimum(tile_expert, N_EXPERTS - 1)
    n_used = (ends[-1] // tm).astype(jnp.int32)[None]
    tile_expert = jnp.where(tile_start < ends[-1], tile_expert, tile_expert[jnp.maximum(n_used[0] - 1, 0)])
    return tok_of_slot, tile_expert.astype(jnp.int32), n_used, slots


def kernel(x, ln_mix_g, w_in, b_in, q_norm_g, w_uq, kv_norm_g, w_ukv, q_head_g, k_head_g,
           s5_lam_re, s5_lam_im, s5_log_step, s5_b_re, s5_b_im, s5_c_re, s5_c_im, s5_d,
           w_glu, b_glu, w_proj_attn, w_proj_ssm, w_out, ln_ffn_g,
           ffn_w_gate, ffn_w_up, ffn_w_down, router_w, router_b,
           moe_w_gate, moe_w_up, moe_w_down):
    batch, seq, d = x.shape
    depth = w_in.shape[0]
    t = batch * seq
    tm = min(512, seq)
    tq = min(512, seq)
    tm_ffn = min(1024, t)
    chunks_per_seq = seq // S5_CHUNK
    assert chunks_per_seq & (chunks_per_seq - 1) == 0 and seq % tm == 0 and t % tm_ffn == 0
    p = dict(ln_mix_g=ln_mix_g, w_in=w_in, b_in=b_in, q_norm_g=q_norm_g, w_uq=w_uq,
             kv_norm_g=kv_norm_g, w_ukv=w_ukv, q_head_g=q_head_g, k_head_g=k_head_g, s5_d=s5_d,
             w_glu=w_glu, b_glu=b_glu, w_proj_attn=w_proj_attn, w_proj_ssm=w_proj_ssm,
             w_out=w_out, ln_ffn_g=ln_ffn_g)
    rope = _rope_tables(seq)
    x2d = x.reshape(t, d)
    for l in range(depth):
        lp = _layer_params(l, p, rope)
        q, k, v, u, ga, gb = _inproj(x2d, lp, seq, tm)
        attn = _attention(q, k, v, batch, seq, tq)
        mats = _s5_matrices(s5_lam_re[l], s5_lam_im[l], s5_log_step[l], s5_b_re[l], s5_b_im[l],
                            s5_c_re[l], s5_c_im[l], chunks_per_seq)
        ys = _s5_scan(u, mats, batch, seq)
        j = l // 2
        if l % 2 == 0:
            x2d, h2 = _mix(x2d, attn, ys, u, ga, gb, lp, tm)
            n_tiles = t // tm_ffn
            x2d = _grouped_ffn(
                jnp.zeros((n_tiles,), jnp.int32), jnp.full((1,), n_tiles, jnp.int32), h2,
                ffn_w_gate[j][None], ffn_w_up[j][None], ffn_w_down[j][None], x2d, tm_ffn, 256)
        else:
            rw = jnp.pad(router_w[j], ((0, 0), (0, ROUTER_PAD - N_EXPERTS)))
            rb = jnp.pad(router_b[j][None], ((0, 0), (0, ROUTER_PAD - N_EXPERTS)), constant_values=NEG_BIG)
            x2d, h2, route, counts = _mix(x2d, attn, ys, u, ga, gb, lp, tm, router=(rw, rb))
            tok_of_slot, tile_expert, n_used, slots = _routing(route, counts, tm_ffn)
            n_tiles = tok_of_slot.shape[0] // tm_ffn
            n_calls = EXPERT_CALLS if n_tiles % EXPERT_CALLS == 0 else 1
            per_call = n_tiles // n_calls
            ysort = jnp.zeros((tok_of_slot.shape[0], d), F32)
            for c in range(n_calls):
                lo = c * per_call
                xs = jnp.take(h2, tok_of_slot[lo * tm_ffn:(lo + per_call) * tm_ffn], axis=0, mode="clip")
                ysort = _grouped_ffn(tile_expert[lo:lo + per_call], n_used - lo, xs, moe_w_gate[j],
                                     moe_w_up[j], moe_w_down[j], None, tm_ffn, 512, ybuf=ysort, tile_offset=lo)
            x2d = _combine(x2d, jnp.take(ysort, slots[:, 0], axis=0, mode="clip"),
                           jnp.take(ysort, slots[:, 1], axis=0, mode="clip"), route, tm)
    return x2d.reshape(batch, seq, d)
```

```python
import functools
import math

import jax
import jax.numpy as jnp
from jax import lax
from jax.experimental import pallas as pl
from jax.experimental.pallas import tpu as pltpu

EPS = 1e-6
N_HEADS = 8
QK_NOPE_DIM = 64
QK_ROPE_DIM = 32
QK_HEAD_DIM = QK_NOPE_DIM + QK_ROPE_DIM
V_HEAD_DIM = 64
LANES = 128
SUBLANES = 8
HEAD_PAD = LANES
MXU_DIM = 256
Q_LORA_RANK = 256
KV_LORA_RANK = 128
ROPE_THETA = 10000.0
S5_WIDTH = 512
S5_GROUP = 16
S5_GROUPS = S5_WIDTH // S5_GROUP
S5_STATE = 64
S5_CHUNK = 16
N_EXPERTS = 8
TOP_K = 2
ROUTER_PAD = LANES
EXPERT_CALLS = 8
NEG_BIG = -1e30
LOG2E = 1.4426950408889634

VMEM_LIMIT_BYTES = 56 * 1024 * 1024

F32 = jnp.float32
BF16 = jnp.bfloat16


def _dot(a, b):
    return jnp.dot(a, b, preferred_element_type=F32)


def _lane_tile(a, n):
    return jnp.concatenate([a] * n, axis=1)


def _rms_scale(x, ones_ref):
    ssq = _dot((x * x).astype(BF16), ones_ref[...])
    return lax.rsqrt(ssq * (1.0 / x.shape[1]) + EPS)


def _rms(x, g, ones_ref):
    return x * _lane_tile(_rms_scale(x, ones_ref), x.shape[1] // LANES) * g


def _params(*semantics):
    return pltpu.CompilerParams(dimension_semantics=semantics, vmem_limit_bytes=VMEM_LIMIT_BYTES)


def _full(shape):
    return pl.BlockSpec(shape, lambda *_: (0,) * len(shape))


def _inproj_kernel(x_ref, g_ref, onesd_ref, w1_ref, b1_ref, w2_ref, b2_ref, qng_ref, onesq_ref,
                   kvng_ref, oneskv_ref, wuq_ref, wuqr_ref, wk_ref, wv_ref, vones_ref, hones_ref,
                   qa_ref, qb_ref, ka_ref, kb_ref,
                   q_ref, k_ref, v_ref, u_ref, ga_ref, gb_ref):
    d = x_ref.shape[1]
    h = _rms(x_ref[...], g_ref[...], onesd_ref).astype(BF16)
    p1 = _dot(h, w1_ref[...]) + b1_ref[...]
    p2 = _dot(h, w2_ref[...]) + b2_ref[...]
    ga_ref[...] = jax.nn.sigmoid(p2[:, :d]).astype(BF16)
    gb_ref[...] = jax.nn.sigmoid(p2[:, d:]).astype(BF16)
    c0 = Q_LORA_RANK
    c1 = c0 + KV_LORA_RANK
    c2 = c1 + HEAD_PAD
    c3 = c2 + HEAD_PAD
    u_ref[...] = p1[:, c3:]
    qn = _rms(p1[:, :c0], qng_ref[...], onesq_ref).astype(BF16)
    kvn = _rms(p1[:, c0:c1], kvng_ref[...], oneskv_ref).astype(BF16)
    v_ref[...] = (_dot(kvn, wv_ref[...]) + vones_ref[...]).astype(BF16)

    def head_norm_rope(raw, rot, a_tab, b_tab):
        chunks = []
        for c in range(raw.shape[1] // MXU_DIM):
            blk = raw[:, c * MXU_DIM:(c + 1) * MXU_DIM]
            chunks.append(_dot((blk * blk).astype(BF16), hones_ref[...]))
        rs = lax.rsqrt(jnp.concatenate(chunks, axis=1) * (1.0 / QK_HEAD_DIM) + EPS)
        return rs * (raw * _lane_tile(a_tab, N_HEADS) + rot * _lane_tile(b_tab, N_HEADS))

    q_raw = _dot(qn, wuq_ref[...])
    q_rot = _dot(qn, wuqr_ref[...])
    q_ref[...] = head_norm_rope(q_raw, q_rot, qa_ref[...], qb_ref[...]).astype(BF16)
    k_raw = _dot(kvn, wk_ref[...]) + _lane_tile(p1[:, c1:c2], N_HEADS)
    k_rot = _lane_tile(p1[:, c2:c3], N_HEADS)
    k_ref[...] = head_norm_rope(k_raw, k_rot, ka_ref[...], kb_ref[...]).astype(BF16)


def _inproj(x2d, lp, seq, tm):
    t, d = x2d.shape
    nseq = seq // tm
    hp = N_HEADS * HEAD_PAD
    row = lambda i: (i, 0)
    pos = lambda i: (i % nseq, 0)
    consts = [lp[k] for k in ("ln_mix_g", "ones_d", "w1", "b1", "w2", "b2", "q_norm_g", "ones_q",
                              "kv_norm_g", "ones_kv", "wuq", "wuq_rot", "wk", "wv", "vones", "head_ones")]
    tabs = [lp[k] for k in ("qa", "qb", "ka", "kb")]
    in_specs = ([pl.BlockSpec((tm, d), row)] + [_full(c.shape) for c in consts]
                + [pl.BlockSpec((tm, HEAD_PAD), pos)] * 4)
    wide = lambda n, dt: (jax.ShapeDtypeStruct((t, n), dt), pl.BlockSpec((tm, n), row))
    outs = [wide(hp, BF16), wide(hp, BF16), wide(hp, BF16), wide(S5_WIDTH, F32), wide(d, BF16), wide(d, BF16)]
    return pl.pallas_call(
        _inproj_kernel, grid=(t // tm,), in_specs=in_specs,
        out_specs=tuple(o[1] for o in outs), out_shape=tuple(o[0] for o in outs),
        compiler_params=_params("parallel"), name="inproj",
    )(x2d, *consts, *tabs)


def _attn_kernel(q_ref, k_ref, v_ref, o_ref, m_scr, acc_scr):
    i = pl.program_id(1)
    j = pl.program_id(2)
    tq = q_ref.shape[0]
    tk = k_ref.shape[0]

    @pl.when(j == 0)
    def _():
        m_scr[...] = jnp.full(m_scr.shape, -jnp.inf, F32)
        acc_scr[...] = jnp.zeros(acc_scr.shape, F32)

    def step(masked):
        if masked:
            rows = lax.broadcasted_iota(jnp.int32, (tq, tk), 0)
            cols = lax.broadcasted_iota(jnp.int32, (tq, tk), 1)
            keep = cols <= rows
        for hd in range(N_HEADS):
            sl = slice(hd * HEAD_PAD, (hd + 1) * HEAD_PAD)
            s = lax.dot_general(q_ref[:, sl], k_ref[:, sl], (((1,), (1,)), ((), ())),
                                preferred_element_type=F32)
            if masked:
                s = jnp.where(keep, s, -jnp.inf)
            parts = [s[:, c * LANES:(c + 1) * LANES] for c in range(tk // LANES)]
            blk_max = functools.reduce(jnp.maximum, parts)
            m_prev = m_scr[hd]
            m_new = jnp.maximum(m_prev, jnp.max(blk_max, axis=-1, keepdims=True))
            alpha = jnp.exp2(m_prev - m_new)
            p = jnp.concatenate([jnp.exp2(part - m_new) for part in parts], axis=1).astype(BF16)
            acc_scr[hd] = alpha * acc_scr[hd] + _dot(p, v_ref[:, sl])
            m_scr[hd] = m_new

    @pl.when(j < i)
    def _():
        step(False)

    @pl.when(j == i)
    def _():
        step(True)
        lane = lax.broadcasted_iota(jnp.int32, (tq, HEAD_PAD), 1)
        for pr in range(N_HEADS // 2):
            a0 = acc_scr[2 * pr]
            a1 = acc_scr[2 * pr + 1]
            lo = a0 / pltpu.roll(a0, V_HEAD_DIM, 1)
            hi = pltpu.roll(a1, V_HEAD_DIM, 1) / a1
            o_ref[:, pr * HEAD_PAD:(pr + 1) * HEAD_PAD] = jnp.where(lane < V_HEAD_DIM, lo, hi).astype(BF16)


def _attention(q, k, v, batch, seq, tq):
    nq = seq // tq
    hp = N_HEADS * HEAD_PAD
    vw = N_HEADS * V_HEAD_DIM
    qmap = lambda b, i, j: (b * nq + i, 0)
    kmap = lambda b, i, j: (b * nq + jnp.minimum(j, i), 0)
    return pl.pallas_call(
        _attn_kernel, grid=(batch, nq, nq),
        in_specs=[pl.BlockSpec((tq, hp), qmap), pl.BlockSpec((tq, hp), kmap),
                  pl.BlockSpec((tq, hp), kmap)],
        out_specs=pl.BlockSpec((tq, vw), qmap),
        out_shape=jax.ShapeDtypeStruct((batch * seq, vw), BF16),
        scratch_shapes=[pltpu.VMEM((N_HEADS, tq, HEAD_PAD), F32), pltpu.VMEM((N_HEADS, tq, HEAD_PAD), F32)],
        compiler_params=_params("parallel", "parallel", "arbitrary"), name="attention",
    )(q, k, v)


def _s5_kernel(u_ref, kc_ref, pc_ref, qc_ref, a_ref, y_ref, acc_scr, tcat_scr, p_scr, q_scr):
    ln = S5_CHUNK
    chunks = acc_scr.shape[0]
    half = a_ref.shape[2] // 2
    gpb = LANES // S5_GROUP

    @pl.when(pl.program_id(1) == 0)
    def _():
        def diag_mask(shape, row_div, col_div, col_mod):
            r = lax.broadcasted_iota(jnp.int32, shape, 0) // row_div
            c = (lax.broadcasted_iota(jnp.int32, shape, 1) % col_mod) // col_div
            return r == c

        rep = lambda x: jnp.concatenate([x] * gpb, axis=0)
        mask_t = diag_mask((LANES, LANES), S5_GROUP, S5_GROUP, LANES)
        mask_p = diag_mask((LANES, 2 * half), S5_GROUP, S5_STATE, half)
        mask_q = diag_mask((half, LANES), S5_STATE, S5_GROUP, LANES)
        for d in range(ln):
            tcat_scr[:, d * LANES:(d + 1) * LANES] = jnp.where(mask_t, rep(kc_ref[0, d]), 0.0).astype(BF16)
            p_scr[d] = jnp.where(mask_p, rep(pc_ref[0, d]), 0.0).astype(BF16)
            qc = qc_ref[0, d]
            q_scr[:half, d * LANES:(d + 1) * LANES] = jnp.where(mask_q, rep(qc[:S5_STATE]), 0.0).astype(BF16)
            q_scr[half:, d * LANES:(d + 1) * LANES] = jnp.where(mask_q, rep(qc[S5_STATE:]), 0.0).astype(BF16)

    z = jnp.zeros((chunks, 2 * half), F32)
    for s in range(ln):
        us = u_ref[pl.ds(s, chunks, stride=ln), :].astype(BF16)
        contrib = _dot(us, tcat_scr[:, :(ln - s) * LANES])
        if s == 0:
            acc_scr[...] = contrib
        else:
            acc_scr[:, s * LANES:] += contrib
        z = z + _dot(us, p_scr[s])
    row = lax.broadcasted_iota(jnp.int32, z.shape, 0)
    a = a_ref[0]
    off = 1
    lvl = 0
    while off < chunks:
        prev = pltpu.roll(z, off, 0)
        contrib = prev * a[2 * lvl:2 * lvl + 1] + pltpu.roll(prev, half, 1) * a[2 * lvl + 1:2 * lvl + 2]
        z = z + jnp.where(row >= off, contrib, 0.0)
        off *= 2
        lvl += 1
    x_start = jnp.where(row >= 1, pltpu.roll(z, 1, 0), 0.0)
    y = acc_scr[...] + _dot(x_start.astype(BF16), q_scr[...])
    for t in range(ln):
        y_ref[pl.ds(t, chunks, stride=ln), :] = y[:, t * LANES:(t + 1) * LANES]


def _s5_scan(u, mats, batch, seq):
    t, width = u.shape
    kc, pc, qc, a = mats
    nblk = width // LANES
    state_w = a.shape[2]
    blk = lambda arr: pl.BlockSpec((1,) + arr.shape[1:], lambda g, b: (g,) + (0,) * (arr.ndim - 1))
    act = pl.BlockSpec((seq, LANES), lambda g, b: (b, g))
    return pl.pallas_call(
        _s5_kernel, grid=(nblk, batch),
        in_specs=[act, blk(kc), blk(pc), blk(qc), blk(a)],
        out_specs=act, out_shape=jax.ShapeDtypeStruct((t, width), F32),
        scratch_shapes=[pltpu.VMEM((seq // S5_CHUNK, S5_CHUNK * LANES), F32),
                        pltpu.VMEM((LANES, S5_CHUNK * LANES), BF16),
                        pltpu.VMEM((S5_CHUNK, LANES, state_w), BF16),
                        pltpu.VMEM((state_w, S5_CHUNK * LANES), BF16)],
        compiler_params=_params("parallel", "arbitrary"), name="s5_scan",
    )(u, kc, pc, qc, a)


def _s5_matrices(lam_re, lam_im, log_step, b_re, b_im, c_re, c_im, chunks_per_seq):
    hi = lax.Precision.HIGHEST
    ln = S5_CHUNK
    gpb = LANES // S5_GROUP
    nblk = S5_GROUPS // gpb
    step = jnp.exp(log_step)[:, None]
    ar = lam_re * step
    ai = lam_im * step
    lbr = jnp.exp(ar) * jnp.cos(ai)
    lbi = jnp.exp(ar) * jnp.sin(ai)
    den = lam_re * lam_re + lam_im * lam_im
    cr = ((lbr - 1.0) * lam_re + lbi * lam_im) / den
    ci = (lbi * lam_re - (lbr - 1.0) * lam_im) / den
    bbr = cr[..., None] * b_re - ci[..., None] * b_im
    bbi = cr[..., None] * b_im + ci[..., None] * b_re
    dd = jnp.arange(ln + 1, dtype=F32)[:, None, None]
    pr = jnp.exp(dd * ar) * jnp.cos(dd * ai)
    pi = jnp.exp(dd * ar) * jnp.sin(dd * ai)
    tr = pr[:ln, :, :, None] * bbr - pi[:ln, :, :, None] * bbi
    ti = pr[:ln, :, :, None] * bbi + pi[:ln, :, :, None] * bbr
    kd = (jnp.einsum("gcp,dgpe->dgce", c_re, tr, precision=hi)
          - jnp.einsum("gcp,dgpe->dgce", c_im, ti, precision=hi))
    def compact(x):
        r, c = x.shape[-2:]
        return x.reshape(ln, nblk, gpb, r, c).transpose(1, 0, 4, 2, 3).reshape(nblk, ln, c, gpb * r)

    kc = compact(kd)
    pc = jnp.concatenate([compact(tr[::-1]), compact(ti[::-1])], axis=-1)
    q_re = c_re[None] * pr[1:, :, None, :] - c_im[None] * pi[1:, :, None, :]
    q_im = -(c_re[None] * pi[1:, :, None, :] + c_im[None] * pr[1:, :, None, :])
    qc = jnp.concatenate([compact(q_re), compact(q_im)], axis=2)
    levels = int(math.log2(chunks_per_seq))
    e = (ln * (2.0 ** jnp.arange(levels, dtype=F32)))[:, None, None]
    er = (jnp.exp(e * ar) * jnp.cos(e * ai)).reshape(levels, nblk, gpb * S5_STATE)
    ei = (jnp.exp(e * ar) * jnp.sin(e * ai)).reshape(levels, nblk, gpb * S5_STATE)
    a = jnp.stack([jnp.concatenate([er, er], -1), jnp.concatenate([-ei, ei], -1)], axis=1)
    a = a.transpose(2, 0, 1, 3).reshape(nblk, 2 * levels, 2 * gpb * S5_STATE)
    return kc, pc, qc, a


def _split_bf16(a):
    hi = a.astype(BF16)
    return hi, (a - hi.astype(F32)).astype(BF16)


def _mix_kernel(*refs, routed):
    (x_ref, attn_ref, ys_ref, u_ref, ga_ref, gb_ref, d_ref, wglu_ref, bglu_ref, wpa_ref, wps_ref,
     wout_ref, lng_ref, onesd_ref) = refs[:14]
    if routed:
        rw_ref, rb_ref, xo_ref, h_ref, route_ref, cnt_ref, cnt_scr = refs[14:]
    else:
        xo_ref, h_ref = refs[14:]
    y = ys_ref[...] + d_ref[...] * u_ref[...]
    y = jax.nn.gelu(y, approximate=True)
    ssm = y * jax.nn.sigmoid(_dot(y.astype(BF16), wglu_ref[...]) + bglu_ref[...])
    pa = _dot(attn_ref[...], wpa_ref[...])
    ps = _dot(ssm.astype(BF16), wps_ref[...])
    mix = ga_ref[...].astype(F32) * pa + gb_ref[...].astype(F32) * ps
    xn = x_ref[...] + _dot(mix.astype(BF16), wout_ref[...])
    xo_ref[...] = xn
    h2 = _rms(xn, lng_ref[...], onesd_ref)
    if not routed:
        h_ref[...] = h2.astype(BF16)
        return
    h_ref[...] = h2
    h_hi, h_lo = _split_bf16(h2)
    r_hi, r_lo = _split_bf16(rw_ref[...])
    logits = _dot(h_hi, r_hi) + _dot(h_lo, r_hi) + _dot(h_hi, r_lo) + rb_ref[...]
    lane = lax.broadcasted_iota(jnp.int32, logits.shape, 1).astype(F32)
    m1 = jnp.max(logits, axis=-1, keepdims=True)
    i1 = jnp.min(jnp.where(logits == m1, lane, float(ROUTER_PAD)), axis=-1, keepdims=True)
    rest = jnp.where(lane == i1, NEG_BIG, logits)
    m2 = jnp.max(rest, axis=-1, keepdims=True)
    i2 = jnp.min(jnp.where(rest == m2, lane, float(ROUTER_PAD)), axis=-1, keepdims=True)
    e2 = jnp.exp(m2 - m1)
    w1 = 1.0 / (1.0 + e2)
    @pl.when(pl.program_id(0) == 0)
    def _():
        cnt_scr[...] = jnp.zeros(cnt_scr.shape, F32)

    pick1 = lane == i1
    pick2 = lane == i2
    onehot = jnp.where(pick1, 1.0, 0.0) + jnp.where(pick2, 1.0, 0.0)
    tmr = onehot.shape[0]
    earlier = (lax.broadcasted_iota(jnp.int32, (tmr, tmr), 1)
               < lax.broadcasted_iota(jnp.int32, (tmr, tmr), 0))
    ranks = _dot(jnp.where(earlier, 1.0, 0.0).astype(BF16), onehot.astype(BF16)) + cnt_scr[0:1, :]
    r1 = jnp.sum(jnp.where(pick1, ranks, 0.0), axis=-1, keepdims=True)
    r2 = jnp.sum(jnp.where(pick2, ranks, 0.0), axis=-1, keepdims=True)
    cnt_scr[...] = cnt_scr[...] + jnp.sum(onehot, axis=0, keepdims=True)
    cnt_ref[...] = cnt_scr[...]
    route_ref[...] = jnp.where(lane == 0.0, i1, jnp.where(lane == 1.0, i2, jnp.where(
        lane == 2.0, w1, jnp.where(lane == 3.0, e2 * w1, jnp.where(
            lane == 4.0, r1, jnp.where(lane == 5.0, r2, 0.0))))))


def _mix(x2d, attn, ys, u, ga, gb, lp, tm, router=None):
    t, d = x2d.shape
    row = lambda i: (i, 0)
    routed = router is not None
    consts = [lp[k] for k in ("s5_d", "w_glu", "b_glu", "w_proj_attn", "w_proj_ssm", "w_out",
                              "ln_ffn_g", "ones_d")]
    if routed:
        consts += list(router)
    acts = [x2d, attn, ys, u, ga, gb]
    in_specs = [pl.BlockSpec((tm, a.shape[1]), row) for a in acts] + [_full(c.shape) for c in consts]
    if routed:
        out_shape = [jax.ShapeDtypeStruct((t, d), F32), jax.ShapeDtypeStruct((t, d), F32),
                     jax.ShapeDtypeStruct((t, ROUTER_PAD), F32)]
    else:
        out_shape = [jax.ShapeDtypeStruct((t, d), F32), jax.ShapeDtypeStruct((t, d), BF16)]
    out_specs = [pl.BlockSpec((tm, o.shape[1]), row) for o in out_shape]
    scratch = []
    if routed:
        out_shape.append(jax.ShapeDtypeStruct((SUBLANES, ROUTER_PAD), F32))
        out_specs.append(_full((SUBLANES, ROUTER_PAD)))
        scratch.append(pltpu.VMEM((SUBLANES, ROUTER_PAD), F32))
    return pl.pallas_call(
        functools.partial(_mix_kernel, routed=routed), grid=(t // tm,), in_specs=in_specs,
        out_specs=tuple(out_specs), out_shape=tuple(out_shape), scratch_shapes=scratch,
        compiler_params=_params("arbitrary" if routed else "parallel"),
        name="mix_routed" if routed else "mix",
    )(*acts, *consts)


def _ffn_kernel(te_ref, nu_ref, *refs, routed):
    if routed:
        x_ref, wg_ref, wu_ref, wd_ref, _, o_ref, acc_scr, x_scr = refs
    else:
        x_ref, wg_ref, wu_ref, wd_ref, res_ref, o_ref, acc_scr = refs
    i = pl.program_id(0)
    j = pl.program_id(1)
    last = pl.num_programs(1) - 1
    used = i < nu_ref[0]

    if routed:
        @pl.when(jnp.logical_and(used, j == 0))
        def _():
            x_scr[...] = x_ref[...].astype(BF16)

    @pl.when(used)
    def _():
        x = x_scr[...] if routed else x_ref[...]
        g = _dot(x, wg_ref[0].astype(BF16))
        a = (g * jax.nn.sigmoid(g) * _dot(x, wu_ref[0].astype(BF16))).astype(BF16)
        part = _dot(a, wd_ref[0].astype(BF16))

        @pl.when(j == 0)
        def _():
            acc_scr[...] = part

        @pl.when(j > 0)
        def _():
            acc_scr[...] += part

    @pl.when(jnp.logical_and(used, j == last))
    def _():
        if routed:
            o_ref[...] = acc_scr[...]
        else:
            o_ref[...] = acc_scr[...] + res_ref[...]

    @pl.when(jnp.logical_and(jnp.logical_not(used), j == last))
    def _():
        o_ref[...] = jnp.zeros(o_ref.shape, o_ref.dtype)


def _grouped_ffn(tile_expert, n_used, xs, wg, wu, wd, res, tm, tf, ybuf=None, tile_offset=0):
    routed = res is None
    n = xs.shape[0]
    d = wg.shape[1]
    nff = wg.shape[2] // tf

    def jeff(i, j, nu):
        return jnp.where(i < nu[0], j, nff - 1)

    row = lambda i, j, te, nu: (i, 0)
    in_specs = [
        pl.BlockSpec((tm, xs.shape[1]), row),
        pl.BlockSpec((1, d, tf), lambda i, j, te, nu: (te[i], 0, jeff(i, j, nu))),
        pl.BlockSpec((1, d, tf), lambda i, j, te, nu: (te[i], 0, jeff(i, j, nu))),
        pl.BlockSpec((1, tf, d), lambda i, j, te, nu: (te[i], jeff(i, j, nu), 0)),
    ]
    args = [xs, wg, wu, wd]
    scratch = [pltpu.VMEM((tm, d), F32)]
    aliases = {}
    if routed:
        scratch.append(pltpu.VMEM((tm, d), BF16))
        in_specs.append(pl.BlockSpec(memory_space=pl.ANY))
        args.append(ybuf)
        aliases = {2 + len(args) - 1: 0}
        out_rows = ybuf.shape[0]
    else:
        in_specs.append(pl.BlockSpec((tm, d), row))
        args.append(res)
        out_rows = n
    grid_spec = pltpu.PrefetchScalarGridSpec(
        num_scalar_prefetch=2, grid=(n // tm, nff), in_specs=in_specs,
        out_specs=pl.BlockSpec((tm, d), lambda i, j, te, nu: (i + tile_offset, 0)), scratch_shapes=scratch)
    return pl.pallas_call(
        functools.partial(_ffn_kernel, routed=routed), grid_spec=grid_spec,
        out_shape=jax.ShapeDtypeStruct((out_rows, d), F32), input_output_aliases=aliases,
        compiler_params=_params("parallel", "arbitrary"),
        name="ffn_experts" if routed else "ffn_dense",
    )(tile_expert, n_used, *args)


def _combine_kernel(x_ref, y0_ref, y1_ref, route_ref, o_ref):
    r = route_ref[...]
    w0 = r[:, 2:3]
    w1 = r[:, 3:4]
    o_ref[...] = x_ref[...] + w0 * y0_ref[...] + w1 * y1_ref[...]


def _combine(x2d, y0, y1, route, tm):
    t, d = x2d.shape
    row = lambda i: (i, 0)
    return pl.pallas_call(
        _combine_kernel, grid=(t // tm,),
        in_specs=[pl.BlockSpec((tm, d), row), pl.BlockSpec((tm, d), row),
                  pl.BlockSpec((tm, d), row), pl.BlockSpec((tm, ROUTER_PAD), row)],
        out_specs=pl.BlockSpec((tm, d), row), out_shape=jax.ShapeDtypeStruct((t, d), F32),
        compiler_params=_params("parallel"), name="combine",
    )(x2d, y0, y1, route)


def _rope_tables(seq):
    half = QK_ROPE_DIM // 2
    pos = jnp.arange(seq, dtype=F32)
    inv = jnp.power(ROPE_THETA, -jnp.arange(0, QK_ROPE_DIM, 2, dtype=F32) / QK_ROPE_DIM)
    ang = pos[:, None] * inv[None, :]
    cos, sin = jnp.cos(ang), jnp.sin(ang)
    tail = jnp.zeros((seq, HEAD_PAD - QK_HEAD_DIM), F32)
    c = jnp.concatenate([jnp.ones((seq, QK_NOPE_DIM), F32), cos, cos, tail], axis=1)
    s = jnp.concatenate([jnp.zeros((seq, QK_NOPE_DIM), F32), -sin, sin, tail], axis=1)
    return c, s


def _rope_partner(a):
    half = QK_ROPE_DIM // 2
    lead = a.shape[:-1]
    a = a.reshape(lead + (-1, HEAD_PAD))
    x1 = a[..., QK_NOPE_DIM:QK_NOPE_DIM + half]
    x2 = a[..., QK_NOPE_DIM + half:QK_HEAD_DIM]
    out = jnp.concatenate([jnp.zeros_like(a[..., :QK_NOPE_DIM]), x2, x1,
                           jnp.zeros_like(a[..., QK_HEAD_DIM:])], axis=-1)
    return out.reshape(lead + (-1,))


def _pad_heads(w, per_head, place_at=0):
    k = w.shape[0]
    w = w.reshape(k, N_HEADS, per_head)
    w = jnp.pad(w, ((0, 0), (0, 0), (place_at, HEAD_PAD - per_head - place_at)))
    return w.reshape(k, N_HEADS * HEAD_PAD)


def _layer_params(l, p, rope):
    d = p["w_in"].shape[1]
    w_in, b_in = p["w_in"][l], p["b_in"][l]
    o = [0, Q_LORA_RANK, Q_LORA_RANK + KV_LORA_RANK, Q_LORA_RANK + KV_LORA_RANK + QK_ROPE_DIM]
    o.append(o[3] + S5_WIDTH)
    o.append(o[4] + d)
    pad_rope = lambda a: jnp.pad(a, ((0, 0), (QK_NOPE_DIM, HEAD_PAD - QK_HEAD_DIM)))
    w_kr, b_kr = pad_rope(w_in[:, o[2]:o[3]]), pad_rope(b_in[None, o[2]:o[3]])
    w1 = jnp.concatenate([w_in[:, o[0]:o[2]], w_kr, _rope_partner(w_kr), w_in[:, o[3]:o[4]]], axis=1)
    b1 = jnp.concatenate([b_in[None, o[0]:o[2]], b_kr, _rope_partner(b_kr), b_in[None, o[3]:o[4]]], axis=1)
    w_ukv = p["w_ukv"][l].reshape(KV_LORA_RANK, N_HEADS, QK_NOPE_DIM + V_HEAD_DIM)
    wuq = _pad_heads(p["w_uq"][l], QK_HEAD_DIM)
    pad_g = lambda g: jnp.pad(g[None, :], ((0, 0), (0, HEAD_PAD - QK_HEAD_DIM)))
    qg, kg = pad_g(p["q_head_g"][l]), pad_g(p["k_head_g"][l])
    cos, sin = rope
    qscale = QK_HEAD_DIM ** -0.5 * LOG2E
    head_ones = jnp.kron(jnp.eye(MXU_DIM // HEAD_PAD, dtype=F32), jnp.ones((HEAD_PAD, HEAD_PAD), F32))
    return {
        "ln_mix_g": p["ln_mix_g"][l][None], "ones_d": jnp.ones((d, LANES), BF16),
        "w1": w1.astype(BF16), "b1": b1, "w2": w_in[:, o[4]:].astype(BF16), "b2": b_in[None, o[4]:],
        "q_norm_g": p["q_norm_g"][l][None], "ones_q": jnp.ones((Q_LORA_RANK, LANES), BF16),
        "kv_norm_g": p["kv_norm_g"][l][None], "ones_kv": jnp.ones((KV_LORA_RANK, LANES), BF16),
        "wuq": wuq.astype(BF16), "wuq_rot": _rope_partner(wuq).astype(BF16),
        "wk": _pad_heads(w_ukv[:, :, :QK_NOPE_DIM].reshape(KV_LORA_RANK, -1), QK_NOPE_DIM).astype(BF16),
        "wv": _pad_heads(w_ukv[:, :, QK_NOPE_DIM:].reshape(KV_LORA_RANK, -1), V_HEAD_DIM).astype(BF16),
        "vones": _pad_heads(jnp.ones((1, N_HEADS * (HEAD_PAD - V_HEAD_DIM)), F32),
                            HEAD_PAD - V_HEAD_DIM, place_at=V_HEAD_DIM),
        "head_ones": head_ones.astype(BF16),
        "qa": qg * cos * qscale, "qb": _rope_partner(qg) * sin * qscale,
        "ka": kg * cos, "kb": _rope_partner(kg) * sin,
        "s5_d": p["s5_d"][l][None], "w_glu": p["w_glu"][l].astype(BF16), "b_glu": p["b_glu"][l][None],
        "w_proj_attn": p["w_proj_attn"][l].astype(BF16), "w_proj_ssm": p["w_proj_ssm"][l].astype(BF16),
        "w_out": p["w_out"][l].astype(BF16), "ln_ffn_g": p["ln_ffn_g"][l][None],
    }


def _routing(route, counts, tm):
    t = route.shape[0]
    e = route[:, :TOP_K].astype(jnp.int32)
    rank = route[:, 2 * TOP_K:3 * TOP_K].astype(jnp.int32)
    cnt = counts[0, :N_EXPERTS].astype(jnp.int32)
    padded = ((cnt + tm - 1) // tm) * tm
    ends = jnp.cumsum(padded)
    starts = ends - padded
    pick = e[:, :, None] == jnp.arange(N_EXPERTS)[None, None, :]
    slots = rank + jnp.sum(jnp.where(pick, starts[None, None, :], 0), axis=-1)
    n_slots = TOP_K * t + N_EXPERTS * tm
    tok = jnp.broadcast_to(jnp.arange(t, dtype=jnp.int32)[:, None], (t, TOP_K))
    tok_of_slot = jnp.zeros((n_slots,), jnp.int32).at[slots.reshape(-1)].set(
        tok.reshape(-1), unique_indices=True)
    tile_start = jnp.arange(n_slots // tm, dtype=jnp.int32) * tm
    tile_expert = jnp.sum((tile_start[:, None] >= ends[None, :]).astype(jnp.int32), axis=1)
    tile_expert = jnp.minimum(tile_expert, N_EXPERTS - 1)
    n_used = (ends[-1] // tm).astype(jnp.int32)[None]
    tile_expert = jnp.where(tile_start < ends[-1], tile_expert, tile_expert[jnp.maximum(n_used[0] - 1, 0)])
    return tok_of_slot, tile_expert.astype(jnp.int32), n_used, slots


def kernel(x, ln_mix_g, w_in, b_in, q_norm_g, w_uq, kv_norm_g, w_ukv, q_head_g, k_head_g,
           s5_lam_re, s5_lam_im, s5_log_step, s5_b_re, s5_b_im, s5_c_re, s5_c_im, s5_d,
           w_glu, b_glu, w_proj_attn, w_proj_ssm, w_out, ln_ffn_g,
           ffn_w_gate, ffn_w_up, ffn_w_down, router_w, router_b,
           moe_w_gate, moe_w_up, moe_w_down):
    batch, seq, d = x.shape
    depth = w_in.shape[0]
    t = batch * seq
    tm = min(512, seq)
    tq = min(512, seq)
    tm_ffn = min(1024, t)
    chunks_per_seq = seq // S5_CHUNK
    assert chunks_per_seq & (chunks_per_seq - 1) == 0 and seq % tm == 0 and t % tm_ffn == 0
    p = dict(ln_mix_g=ln_mix_g, w_in=w_in, b_in=b_in, q_norm_g=q_norm_g, w_uq=w_uq,
             kv_norm_g=kv_norm_g, w_ukv=w_ukv, q_head_g=q_head_g, k_head_g=k_head_g, s5_d=s5_d,
             w_glu=w_glu, b_glu=b_glu, w_proj_attn=w_proj_attn, w_proj_ssm=w_proj_ssm,
             w_out=w_out, ln_ffn_g=ln_ffn_g)
    rope = _rope_tables(seq)
    x2d = x.reshape(t, d)
    for l in range(depth):
        lp = _layer_params(l, p, rope)
        q, k, v, u, ga, gb = _inproj(x2d, lp, seq, tm)
        attn = _attention(q, k, v, batch, seq, tq)
        mats = _s5_matrices(s5_lam_re[l], s5_lam_im[l], s5_log_step[l], s5_b_re[l], s5_b_im[l],
                            s5_c_re[l], s5_c_im[l], chunks_per_seq)
        ys = _s5_scan(u, mats, batch, seq)
        j = l // 2
        if l % 2 == 0:
            x2d, h2 = _mix(x2d, attn, ys, u, ga, gb, lp, tm)
            n_tiles = t // tm_ffn
            x2d = _grouped_ffn(
                jnp.zeros((n_tiles,), jnp.int32), jnp.full((1,), n_tiles, jnp.int32), h2,
                ffn_w_gate[j][None], ffn_w_up[j][None], ffn_w_down[j][None], x2d, tm_ffn, 256)
        else:
            rw = jnp.pad(router_w[j], ((0, 0), (0, ROUTER_PAD - N_EXPERTS)))
            rb = jnp.pad(router_b[j][None], ((0, 0), (0, ROUTER_PAD - N_EXPERTS)), constant_values=NEG_BIG)
            x2d, h2, route, counts = _mix(x2d, attn, ys, u, ga, gb, lp, tm, router=(rw, rb))
            tok_of_slot, tile_expert, n_used, slots = _routing(route, counts, tm_ffn)
            n_tiles = tok_of_slot.shape[0] // tm_ffn
            n_calls = EXPERT_CALLS if n_tiles % EXPERT_CALLS == 0 else 1
            per_call = n_tiles // n_calls
            ysort = jnp.zeros((tok_of_slot.shape[0], d), F32)
            for c in range(n_calls):
                lo = c * per_call
                xs = jnp.take(h2, tok_of_slot[lo * tm_ffn:(lo + per_call) * tm_ffn], axis=0, mode="clip")
                ysort = _grouped_ffn(tile_expert[lo:lo + per_call], n_used - lo, xs, moe_w_gate[j],
                                     moe_w_up[j], moe_w_down[j], None, tm_ffn, 512, ybuf=ysort, tile_offset=lo)
            x2d = _combine(x2d, jnp.take(ysort, slots[:, 0], axis=0, mode="clip"),
                           jnp.take(ysort, slots[:, 1], axis=0, mode="clip"), route, tm)
    return x2d.reshape(batch, seq, d)
```
